```python
import math
import jax, jax.numpy as jnp
from jax import lax
import numpy as np

D_MODEL = 2048
BATCH = 2
SEQ = 4096
DEPTH = 4
DEC_BATCH = 8
DEC_SEQ = 4
PAST_LEN = 16384
PAGE_SIZE = 128

N_MIXERS = 4
N_POOL_LAYERS = (DEPTH + 3) // N_MIXERS
N_SB_LAYERS = (DEPTH + 2) // N_MIXERS
N_RET_LAYERS = (DEPTH + 1) // N_MIXERS
N_CONV_LAYERS = DEPTH // N_MIXERS

D_FF = 5632
NORM_EPS = 1e-6

POOL_WINDOWS = (2, 4, 8, 16)
POOL_GROUPS = len(POOL_WINDOWS)
POOL_GROUP_DIM = D_MODEL // POOL_GROUPS
POOL_BUF = max(POOL_WINDOWS) - 1

SB_HEADS = 16
SB_HEAD_DIM = D_MODEL // SB_HEADS
SB_BLOCK = 128
SB_BIAS_INIT = -8.0

RET_HEADS = 8
RET_DK = D_MODEL // RET_HEADS
RET_DV = 2 * RET_DK
RET_CHUNK = 128
ROPE_BASE = 10000.0

CONV_WIDTH = 3
CONV_BUF = CONV_WIDTH - 1

kernel_name = 'hybrid_pool_sb_ret_conv_decoder_step'


def rms_norm(x, g):
    xf = x.astype(jnp.float32)
    y = xf * lax.rsqrt(jnp.mean(xf * xf, axis=-1, keepdims=True) + NORM_EPS)
    return (y * g.astype(jnp.float32)).astype(x.dtype)


def swiglu(x, w13, w2):
    a, b = jnp.split(x @ w13, 2, axis=-1)
    return (jax.nn.silu(a) * b) @ w2


def pool_mixer(xn, buf, pos0, w_grp, scale):
    bsz, s = xn.shape[0], xn.shape[1]
    xe = jnp.concatenate([buf.astype(xn.dtype), xn], axis=1)
    cs = jnp.cumsum(xe.astype(jnp.float32), axis=1)
    cs = jnp.pad(cs, ((0, 0), (1, 0), (0, 0)))
    end = cs[:, POOL_BUF + 1:]
    pos = pos0 + jnp.arange(s, dtype=jnp.int32)
    parts = []
    for g, w in enumerate(POOL_WINDOWS):
        sl = slice(g * POOL_GROUP_DIM, (g + 1) * POOL_GROUP_DIM)
        start = cs[:, POOL_BUF + 1 - w: POOL_BUF + 1 - w + s, sl]
        cnt = jnp.minimum(w, pos + 1).astype(jnp.float32)[None, :, None]
        parts.append((end[..., sl] - start) / cnt)
    pooled = jnp.concatenate(parts, axis=-1)
    d = (pooled - xn.astype(jnp.float32)).astype(xn.dtype)
    d = d.reshape(bsz, s, POOL_GROUPS, POOL_GROUP_DIM)
    y = jnp.einsum('bsgc,gce->bsge', d, w_grp).reshape(bsz, s, D_MODEL)
    return y * scale, xe[:, -POOL_BUF:]


def sb_project(xn, w_qkv, g_q, g_k):
    bsz, s = xn.shape[0], xn.shape[1]
    q, k, v = jnp.split(xn @ w_qkv, 3, axis=-1)
    q = rms_norm(q.reshape(bsz, s, SB_HEADS, SB_HEAD_DIM), g_q)
    k = rms_norm(k.reshape(bsz, s, SB_HEADS, SB_HEAD_DIM), g_k)
    v = v.reshape(bsz, s, SB_HEADS, SB_HEAD_DIM)
    return q, k, v


def stick_breaking(q, k, v, pos0, bias):
    bsz, sq, h, hd = q.shape
    sk = k.shape[1]
    blk = min(SB_BLOCK, sq)
    nb = sq // blk
    qb = q.reshape(bsz, nb, blk, h, hd).swapaxes(0, 1)
    qpos = (pos0 + jnp.arange(sq, dtype=jnp.int32)).reshape(nb, blk)
    kpos = jnp.arange(sk, dtype=jnp.int32)
    scale = 1.0 / math.sqrt(hd)
    b_h = bias.astype(jnp.float32)[None, :, None, None]

    def one_block(args):
        qblk, pb = args
        z = jnp.einsum('bqhd,bkhd->bhqk', qblk, k, preferred_element_type=jnp.float32) * scale + b_h
        mask = kpos[None, :] < pb[:, None]
        ls = jnp.where(mask, jax.nn.log_sigmoid(-z), 0.0)
        log_a = jnp.where(mask, z + lax.cumsum(ls, axis=3, reverse=True), -jnp.inf)
        a = jnp.exp(log_a)
        return jnp.einsum('bhqk,bkhd->bqhd', a.astype(v.dtype), v)

    o = lax.map(one_block, (qb, qpos))
    return o.swapaxes(0, 1).reshape(bsz, sq, h * hd)


def rotary(x, pos):
    d = x.shape[-1]
    inv = ROPE_BASE ** (-jnp.arange(0, d, 2, dtype=jnp.float32) / d)
    ang = pos.astype(jnp.float32)[:, None] * inv[None, :]
    cos = jnp.cos(ang)[None, :, None, :]
    sin = jnp.sin(ang)[None, :, None, :]
    x1, x2 = jnp.split(x.astype(jnp.float32), 2, axis=-1)
    return jnp.concatenate([x1 * cos - x2 * sin, x1 * sin + x2 * cos], axis=-1)


def retention(xn, s0, pos0, w_qkvg, w_o):
    bsz, s = xn.shape[0], xn.shape[1]
    hk, hv = RET_HEADS * RET_DK, RET_HEADS * RET_DV
    q, k, v, g = jnp.split(xn @ w_qkvg, [hk, 2 * hk, 2 * hk + hv], axis=-1)
    pos = pos0 + jnp.arange(s, dtype=jnp.int32)
    q = rotary(q.reshape(bsz, s, RET_HEADS, RET_DK), pos)
    k = rotary(k.reshape(bsz, s, RET_HEADS, RET_DK), pos) * (RET_DK ** -0.5)
    v = v.reshape(bsz, s, RET_HEADS, RET_DV).astype(jnp.float32)
    log_gamma = jnp.log1p(-(2.0 ** (-5.0 - jnp.arange(RET_HEADS, dtype=jnp.float32))))
    c = min(RET_CHUNK, s)
    n = s // c
    idx = jnp.arange(c, dtype=jnp.float32)
    diff = idx[:, None] - idx[None, :]
    decay_in = jnp.exp(jnp.where(diff >= 0, diff[None] * log_gamma[:, None, None], -jnp.inf))
    decay_q = jnp.exp((idx + 1.0)[:, None] * log_gamma[None, :])[None, :, :, None]
    decay_k = jnp.exp((c - 1.0 - idx)[:, None] * log_gamma[None, :])[None, :, :, None]
    decay_c = jnp.exp(c * log_gamma)[None, :, None, None]

    def to_chunks(t):
        return t.reshape(bsz, n, c, RET_HEADS, t.shape[-1]).swapaxes(0, 1)

    def step(state, xs):
        qc, kc, vc = xs
        att = jnp.einsum('bihd,bjhd->bhij', qc, kc) * decay_in
        o = jnp.einsum('bhij,bjhe->bihe', att, vc)
        o = o + jnp.einsum('bihd,bhde->bihe', qc, state) * decay_q
        state = state * decay_c + jnp.einsum('bjhd,bjhe->bhde', kc * decay_k, vc)
        return state, o

    s_fin, o = lax.scan(step, s0.astype(jnp.float32), (to_chunks(q), to_chunks(k), to_chunks(v)))
    o = o.swapaxes(0, 1).reshape(bsz, s, RET_HEADS, RET_DV)
    o = o * lax.rsqrt(jnp.mean(o * o, axis=-1, keepdims=True) + NORM_EPS)
    y = (jax.nn.silu(g) * o.reshape(bsz, s, hv).astype(g.dtype)) @ w_o
    return y, s_fin.astype(s0.dtype)


def short_conv(xn, buf, w_in, w_conv, w_out):
    s = xn.shape[1]
    b, c, h = jnp.split(xn @ w_in, 3, axis=-1)
    u = c * h
    ue = jnp.concatenate([buf.astype(u.dtype), u], axis=1)
    conv = w_conv[0] * ue[:, 0:s]
    for t in range(1, CONV_WIDTH):
        conv = conv + w_conv[t] * ue[:, t:t + s]
    return (b * conv) @ w_out, ue[:, -CONV_BUF:]


def gather_pages(cache_layer, page_table):
    pages = jnp.take(cache_layer, page_table, axis=0)
    return pages.reshape(page_table.shape[0], page_table.shape[1] * cache_layer.shape[1],
                         cache_layer.shape[2], cache_layer.shape[3])


def setup_inputs(seed: int = 0) -> dict:
    key = jax.random.key(seed)
    ks = jax.random.split(key, 32)
    n_pages = PAST_LEN // PAGE_SIZE
    n_used = DEC_BATCH * n_pages
    n_pool_pages = n_used + n_used // 4
    f32 = jnp.float32

    def nrm(k, shape, fan_in=None):
        w = jax.random.normal(k, shape, f32)
        return w * (fan_in ** -0.5) if fan_in is not None else w

    def gain(k, shape):
        return 1.0 + 0.02 * jax.random.normal(k, shape, f32)

    page_table = jax.random.permutation(ks[0], n_pool_pages)[:n_used].reshape(DEC_BATCH, n_pages).astype(jnp.int32)
    return {
        'x_prompt': nrm(ks[1], (BATCH, SEQ, D_MODEL)),
        'x_sample': nrm(ks[2], (DEC_BATCH, DEC_SEQ, D_MODEL)),
        'state_pool': nrm(ks[3], (N_POOL_LAYERS, DEC_BATCH, POOL_BUF, D_MODEL)),
        'cache_sb_k': nrm(ks[4], (N_SB_LAYERS, n_pool_pages, PAGE_SIZE, SB_HEADS, SB_HEAD_DIM)),
        'cache_sb_v': nrm(ks[5], (N_SB_LAYERS, n_pool_pages, PAGE_SIZE, SB_HEADS, SB_HEAD_DIM)),
        'page_table': page_table,
        'state_ret': nrm(ks[6], (N_RET_LAYERS, DEC_BATCH, RET_HEADS, RET_DK, RET_DV)),
        'state_conv': nrm(ks[7], (N_CONV_LAYERS, DEC_BATCH, CONV_BUF, D_MODEL)),
        'ffn_norm': gain(ks[8], (DEPTH, 2, D_MODEL)),
        'ffn_w13': nrm(ks[9], (DEPTH, 2, D_MODEL, 2 * D_FF), D_MODEL),
        'ffn_w2': nrm(ks[10], (DEPTH, 2, D_FF, D_MODEL), D_FF),
        'mix_norm': gain(ks[11], (DEPTH, D_MODEL)),
        'pool_w': nrm(ks[12], (N_POOL_LAYERS, POOL_GROUPS, POOL_GROUP_DIM, POOL_GROUP_DIM), POOL_GROUP_DIM),
        'pool_scale': gain(ks[13], (N_POOL_LAYERS, D_MODEL)),
        'sb_w_qkv': nrm(ks[14], (N_SB_LAYERS, D_MODEL, 3 * SB_HEADS * SB_HEAD_DIM), D_MODEL),
        'sb_q_norm': gain(ks[15], (N_SB_LAYERS, SB_HEAD_DIM)),
        'sb_k_norm': gain(ks[16], (N_SB_LAYERS, SB_HEAD_DIM)),
        'sb_bias': SB_BIAS_INIT + 0.1 * jax.random.normal(ks[23], (N_SB_LAYERS, SB_HEADS), f32),
        'sb_w_o': nrm(ks[17], (N_SB_LAYERS, SB_HEADS * SB_HEAD_DIM, D_MODEL), SB_HEADS * SB_HEAD_DIM),
        'ret_w_qkvg': nrm(ks[18], (N_RET_LAYERS, D_MODEL, 2 * RET_HEADS * RET_DK + 2 * RET_HEADS * RET_DV), D_MODEL),
        'ret_w_o': nrm(ks[19], (N_RET_LAYERS, RET_HEADS * RET_DV, D_MODEL), RET_HEADS * RET_DV),
        'conv_w_in': nrm(ks[20], (N_CONV_LAYERS, D_MODEL, 3 * D_MODEL), D_MODEL),
        'conv_w': nrm(ks[21], (N_CONV_LAYERS, CONV_WIDTH, D_MODEL), CONV_WIDTH),
        'conv_w_out': nrm(ks[22], (N_CONV_LAYERS, D_MODEL, D_MODEL), D_MODEL),
    }


def reference(x_prompt, x_sample, state_pool, cache_sb_k, cache_sb_v, page_table, state_ret, state_conv,
              ffn_norm, ffn_w13, ffn_w2, mix_norm, pool_w, pool_scale, sb_w_qkv, sb_q_norm, sb_k_norm,
              sb_bias, sb_w_o, ret_w_qkvg, ret_w_o, conv_w_in, conv_w, conv_w_out):

    def trunk(x, pos0, pool_buf, ret_s0, conv_buf0, paged):
        pools, ks, vs, rets, convs = [], [], [], [], []
        for i in range(DEPTH):
            j, m = i // N_MIXERS, i % N_MIXERS
            x = x + 0.5 * swiglu(rms_norm(x, ffn_norm[i, 0]), ffn_w13[i, 0], ffn_w2[i, 0])
            xn = rms_norm(x, mix_norm[i])
            if m == 0:
                y, buf = pool_mixer(xn, pool_buf[j], pos0, pool_w[j], pool_scale[j])
                pools.append(buf)
            elif m == 1:
                q, k, v = sb_project(xn, sb_w_qkv[j], sb_q_norm[j], sb_k_norm[j])
                ks.append(k)
                vs.append(v)
                if paged:
                    k = jnp.concatenate([gather_pages(cache_sb_k[j], page_table).astype(k.dtype), k], axis=1)
                    v = jnp.concatenate([gather_pages(cache_sb_v[j], page_table).astype(v.dtype), v], axis=1)
                y = stick_breaking(q, k, v, pos0, sb_bias[j]) @ sb_w_o[j]
            elif m == 2:
                y, s_new = retention(xn, ret_s0[j], pos0, ret_w_qkvg[j], ret_w_o[j])
                rets.append(s_new)
            else:
                y, buf = short_conv(xn, conv_buf0[j], conv_w_in[j], conv_w[j], conv_w_out[j])
                convs.append(buf)
            x = x + y
            x = x + 0.5 * swiglu(rms_norm(x, ffn_norm[i, 1]), ffn_w13[i, 1], ffn_w2[i, 1])
        return x, jnp.stack(pools), jnp.stack(ks), jnp.stack(vs), jnp.stack(rets), jnp.stack(convs)

    bp = x_prompt.shape[0]
    y_prompt, pool_p, k_p, v_p, ret_p, conv_p = trunk(
        x_prompt, 0,
        jnp.zeros((N_POOL_LAYERS, bp, POOL_BUF, D_MODEL), x_prompt.dtype),
        jnp.zeros((N_RET_LAYERS, bp, RET_HEADS, RET_DK, RET_DV), state_ret.dtype),
        jnp.zeros((N_CONV_LAYERS, bp, CONV_BUF, D_MODEL), x_prompt.dtype),
        False)
    past_len = page_table.shape[1] * cache_sb_k.shape[2]
    y_sample, pool_s, k_s, v_s, ret_s, conv_s = trunk(
        x_sample, past_len, state_pool, state_ret, state_conv, True)
    return (y_prompt, y_sample, pool_p, pool_s, k_p, v_p, k_s, v_s, ret_p, ret_s, conv_p, conv_s)
```

```python
import functools
import math

import jax
import jax.numpy as jnp
from jax import lax
from jax.experimental import pallas as pl
from jax.experimental.pallas import tpu as pltpu

F32 = jnp.float32
BF16 = jnp.bfloat16

NORM_EPS = 1e-6
POOL_WINDOWS = (2, 4, 8, 16)
POOL_HALO = 16
SB_HEADS = 16
SB_HEAD_DIM = 128
RET_HEADS = 8
RET_DK = 256
RET_DV = 512
ROPE_BASE = 10000.0
CONV_WIDTH = 3
CONV_HALO = 8
SAMPLE_ROWS = 16
SAMPLE_Q_ROWS = 8

VMEM_LIMIT_BYTES = 56 * 1024 * 1024


def _params(*sem):
    return pltpu.CompilerParams(dimension_semantics=sem, vmem_limit_bytes=VMEM_LIMIT_BYTES)


def _rms(x, g):
    ms = jnp.mean(x * x, axis=-1, keepdims=True)
    return x * lax.rsqrt(ms + NORM_EPS) * g


def _dot(a, b):
    return jnp.dot(a, b, preferred_element_type=F32)


def _dot_nt(a, b):
    return lax.dot_general(a, b, (((1,), (1,)), ((), ())), preferred_element_type=F32)


def _dot_tn(a, b):
    return lax.dot_general(a, b, (((0,), (0,)), ((), ())), preferred_element_type=F32)


def _norm_mm_kernel(*refs, n_w, n_extra, epilogue):
    x_ref, g_ref = refs[0], refs[1]
    w_refs = refs[2:2 + n_w]
    extra = refs[2 + n_w:2 + n_w + n_extra]
    outs = refs[2 + n_w + n_extra:-1]
    xn_sc = refs[-1]

    @pl.when(pl.program_id(1) == 0)
    def _():
        xn_sc[...] = _rms(x_ref[...], g_ref[...]).astype(BF16)

    xn = xn_sc[...]
    accs = [_dot(xn, w[...]) for w in w_refs]
    epilogue(accs, extra, outs)


def _norm_mm(x, gain, w, col_offsets, n_cols, epilogue, out_dtypes, extras=(), tm=1024, tn=512,
             name="norm_mm"):
    m, k = x.shape
    tm = min(tm, m)
    assert m % tm == 0 and n_cols % tn == 0 and all(off % tn == 0 for off in col_offsets)
    grid = (m // tm, n_cols // tn)
    in_specs = [pl.BlockSpec((tm, k), lambda i, j: (i, 0)),
                pl.BlockSpec((1, k), lambda i, j: (0, 0))]
    args = [x, gain.reshape(1, k)]
    for off in col_offsets:
        in_specs.append(pl.BlockSpec((k, tn), functools.partial(lambda i, j, o: (0, j + o), o=off // tn)))
        args.append(w)
    for arr, spec in extras:
        in_specs.append(spec)
        args.append(arr)
    out_shape = [jax.ShapeDtypeStruct((m, n_cols), dt) for dt in out_dtypes]
    out_specs = [pl.BlockSpec((tm, tn), lambda i, j: (i, j)) for _ in out_dtypes]
    kern = functools.partial(_norm_mm_kernel, n_w=len(col_offsets), n_extra=len(extras),
                             epilogue=epilogue)
    return pl.pallas_call(
        kern, grid=grid, in_specs=in_specs, out_specs=out_specs, out_shape=out_shape,
        scratch_shapes=[pltpu.VMEM((tm, k), BF16)],
        compiler_params=_params("arbitrary", "arbitrary"), name=name)(*args)


def _ep_swiglu(accs, extra, outs):
    a, b = accs
    outs[0][...] = (a * jax.nn.sigmoid(a) * b).astype(BF16)


def _ep_sb_q(accs, extra, outs, *, n_heads, scale):
    g = extra[0][...]
    for h in range(n_heads):
        sl = slice(h * SB_HEAD_DIM, (h + 1) * SB_HEAD_DIM)
        outs[0][:, sl] = (_rms(accs[0][:, sl], g) * scale).astype(BF16)


def _ep_sb_k(accs, extra, outs, *, n_heads):
    g = extra[0][...]
    for h in range(n_heads):
        sl = slice(h * SB_HEAD_DIM, (h + 1) * SB_HEAD_DIM)
        kn = _rms(accs[0][:, sl], g)
        outs[0][:, sl] = kn
        outs[1][:, sl] = kn.astype(BF16)


def _ep_copy2(accs, extra, outs):
    outs[0][...] = accs[0]
    outs[1][...] = accs[0].astype(BF16)


def _ep_bf16(accs, extra, outs):
    outs[0][...] = accs[0].astype(BF16)


def _ep_silu(accs, extra, outs):
    a = accs[0]
    outs[0][...] = (a * jax.nn.sigmoid(a)).astype(BF16)


def _ep_rotary(accs, extra, outs, *, n_heads, scale):
    cos = extra[0][...]
    sin = extra[1][...]
    half = RET_DK // 2
    for h in range(n_heads):
        x1 = accs[0][:, h * RET_DK:h * RET_DK + half]
        x2 = accs[0][:, h * RET_DK + half:(h + 1) * RET_DK]
        outs[0][:, h * RET_DK:h * RET_DK + half] = ((x1 * cos - x2 * sin) * scale).astype(BF16)
        outs[0][:, h * RET_DK + half:(h + 1) * RET_DK] = ((x1 * sin + x2 * cos) * scale).astype(BF16)


def _mm_res_kernel(a_ref, w_ref, r_ref, o_ref, *, scale):
    o_ref[...] = r_ref[...] + scale * _dot(a_ref[...], w_ref[...])


def _mm_res(a, w, res, scale, tm=1024, tn=512, name="mm_res"):
    m, k = a.shape
    n = w.shape[1]
    tm = min(tm, m)
    assert m % tm == 0 and n % tn == 0
    return pl.pallas_call(
        functools.partial(_mm_res_kernel, scale=scale),
        grid=(m // tm, n // tn),
        in_specs=[pl.BlockSpec((tm, k), lambda i, j: (i, 0)),
                  pl.BlockSpec((k, tn), lambda i, j: (0, j)),
                  pl.BlockSpec((tm, tn), lambda i, j: (i, j))],
        out_specs=pl.BlockSpec((tm, tn), lambda i, j: (i, j)),
        out_shape=jax.ShapeDtypeStruct((m, n), F32),
        compiler_params=_params("arbitrary", "arbitrary"), name=name)(a, w, res)


def _ffn(x, gain, w13, w2, d_ff):
    h = _norm_mm(x, gain, w13, (0, d_ff), d_ff, _ep_swiglu, (BF16,), name="ffn_up")[0]
    return _mm_res(h, w2, x, 0.5, name="ffn_down")


def _pool_kernel(x_ref, g_ref, buf_ref, w_ref, sc_ref, o_ref, xn_ref, xe_sc, *, ts, pos0):
    s = pl.program_id(1)

    @pl.when(s == 0)
    def _():
        xe_sc[0:POOL_HALO, :] = buf_ref[0]

    x = x_ref[0]
    xn = _rms(x, g_ref[...])
    xn_ref[0] = xn
    xe_sc[POOL_HALO:POOL_HALO + ts, :] = xn

    gd = x.shape[1] // len(POOL_WINDOWS)
    pos = pos0 + s * ts + lax.broadcasted_iota(jnp.int32, (ts, 1), 0)
    for gi, win in enumerate(POOL_WINDOWS):
        sl = slice(gi * gd, (gi + 1) * gd)
        acc = xe_sc[:, sl]
        step = 1
        while step < win:
            acc = acc + pltpu.roll(acc, step, 0)
            step *= 2
        cnt = jnp.minimum(win, pos + 1).astype(F32)
        pooled = acc[POOL_HALO:, :] / cnt
        d = (pooled - xn[:, sl]).astype(BF16)
        y = _dot(d, w_ref[gi])
        o_ref[0, :, sl] = x[:, sl] + y * sc_ref[:, sl]

    xe_sc[0:POOL_HALO, :] = xe_sc[ts:ts + POOL_HALO, :]


def _pool_mixer(x, gain, buf, w, scale, pos0, ts):
    b, s, d = x.shape
    ts = min(ts, s)
    assert s % ts == 0
    gd = d // len(POOL_WINDOWS)
    return pl.pallas_call(
        functools.partial(_pool_kernel, ts=ts, pos0=pos0),
        grid=(b, s // ts),
        in_specs=[pl.BlockSpec((1, ts, d), lambda i, j: (i, j, 0)),
                  pl.BlockSpec((1, d), lambda i, j: (0, 0)),
                  pl.BlockSpec((1, POOL_HALO, d), lambda i, j: (i, 0, 0)),
                  pl.BlockSpec((len(POOL_WINDOWS), gd, gd), lambda i, j: (0, 0, 0)),
                  pl.BlockSpec((1, d), lambda i, j: (0, 0))],
        out_specs=[pl.BlockSpec((1, ts, d), lambda i, j: (i, j, 0)),
                   pl.BlockSpec((1, ts, d), lambda i, j: (i, j, 0))],
        out_shape=[jax.ShapeDtypeStruct((b, s, d), F32), jax.ShapeDtypeStruct((b, s, d), F32)],
        scratch_shapes=[pltpu.VMEM((ts + POOL_HALO, d), F32)],
        compiler_params=_params("arbitrary", "arbitrary"), name="pool_mixer",
    )(x, gain.reshape(1, d), buf, w, scale.reshape(1, d))


def _sb_block(q, k, v, bias, tri, carry, acc, mask):
    z = _dot_nt(q, k) + bias
    ls = jnp.minimum(-z, 0.0) - jnp.log1p(jnp.exp(-jnp.abs(z)))
    if mask is not None:
        ls = jnp.where(mask, ls, 0.0)
    hi = ls.astype(BF16)
    lo = (ls - hi.astype(F32)).astype(BF16)
    cs = _dot(hi, tri) + _dot(lo, tri)
    a = jnp.exp(z + cs + carry)
    if mask is not None:
        a = jnp.where(mask, a, 0.0)
    acc = acc + _dot(a.astype(BF16), v)
    carry = carry + jnp.sum(ls, axis=-1, keepdims=True)
    return carry, acc


def _sb_prompt_kernel(bias_ref, q_ref, k_ref, v_ref, tri_ref, o_ref, *, tq):
    h = pl.program_id(1)
    i = pl.program_id(2)
    q = q_ref[0]
    bias = bias_ref[h]
    tri = tri_ref[...]
    rows = lax.broadcasted_iota(jnp.int32, (tq, tq), 0)
    cols = lax.broadcasted_iota(jnp.int32, (tq, tq), 1)
    off = pl.multiple_of(i * tq, tq)
    carry = jnp.zeros((tq, 1), F32)
    acc = jnp.zeros((tq, SB_HEAD_DIM), F32)
    carry, acc = _sb_block(q, k_ref[0, pl.ds(off, tq), :], v_ref[0, pl.ds(off, tq), :], bias, tri,
                           carry, acc, cols < rows)

    def body(n, c):
        o = pl.multiple_of((i - 1 - n) * tq, tq)
        return _sb_block(q, k_ref[0, pl.ds(o, tq), :], v_ref[0, pl.ds(o, tq), :], bias, tri,
                         c[0], c[1], None)

    carry, acc = lax.fori_loop(0, i, body, (carry, acc))
    o_ref[0] = acc.astype(BF16)


def _tri(n):
    r = lax.broadcasted_iota(jnp.int32, (n, n), 0)
    c = lax.broadcasted_iota(jnp.int32, (n, n), 1)
    return (r >= c).astype(BF16)


def _sb_prompt(q, k, v, bias, tq=256):
    b, s, hd_all = q.shape
    tq = min(tq, s)
    assert s % tq == 0
    n_heads = hd_all // SB_HEAD_DIM
    return pl.pallas_call(
        functools.partial(_sb_prompt_kernel, tq=tq),
        grid=(b, n_heads, s // tq),
        in_specs=[pl.BlockSpec(memory_space=pltpu.SMEM),
                  pl.BlockSpec((1, tq, SB_HEAD_DIM), lambda bi, h, i: (bi, i, h)),
                  pl.BlockSpec((1, s, SB_HEAD_DIM), lambda bi, h, i: (bi, 0, h)),
                  pl.BlockSpec((1, s, SB_HEAD_DIM), lambda bi, h, i: (bi, 0, h)),
                  pl.BlockSpec((tq, tq), lambda bi, h, i: (0, 0))],
        out_specs=pl.BlockSpec((1, tq, SB_HEAD_DIM), lambda bi, h, i: (bi, i, h)),
        out_shape=jax.ShapeDtypeStruct((b, s, hd_all), BF16),
        compiler_params=_params("arbitrary", "arbitrary", "arbitrary"), name="sb_prompt",
    )(bias, q, k, v, _tri(tq))


def _sb_sample_kernel(pt_ref, q_ref, brow_ref, kn_ref, vn_ref, kc_ref, vc_ref, tri_ref, o_ref,
                      acc_sc, carry_sc, *, n_heads):
    p = pl.program_id(1)
    page = kn_ref.shape[1]

    @pl.when(p == 0)
    def _():
        acc_sc[...] = jnp.zeros_like(acc_sc)
        carry_sc[...] = jnp.zeros_like(carry_sc)

    def step(k_ref, v_ref, mask):
        carry, acc = _sb_block(q_ref[0], k_ref[0].astype(BF16), v_ref[0].astype(BF16), brow_ref[...],
                               tri_ref[...], carry_sc[...], acc_sc[...], mask)
        acc_sc[...] = acc
        carry_sc[...] = carry

    @pl.when(p == 0)
    def _():
        nrow = q_ref.shape[1]
        t = lax.broadcasted_iota(jnp.int32, (nrow, page), 0) % SAMPLE_Q_ROWS
        sk = lax.broadcasted_iota(jnp.int32, (nrow, page), 1)
        step(kn_ref, vn_ref, sk < t)

    @pl.when(p > 0)
    def _():
        step(kc_ref, vc_ref, None)

    @pl.when(p == pl.num_programs(1) - 1)
    def _():
        for h in range(n_heads):
            o_ref[0, :, h * SB_HEAD_DIM:(h + 1) * SB_HEAD_DIM] = acc_sc[
                h * SAMPLE_Q_ROWS:(h + 1) * SAMPLE_Q_ROWS, h * SB_HEAD_DIM:(h + 1) * SB_HEAD_DIM]


def _sb_sample(q, k_new, v_new, cache_k, cache_v, page_table, bias):
    b, _, hd_all = q.shape
    n_heads = hd_all // SB_HEAD_DIM
    n_pages = page_table.shape[1]
    page = cache_k.shape[1]
    nrow = n_heads * SAMPLE_Q_ROWS
    q8 = q[:, :SAMPLE_Q_ROWS].reshape(b, SAMPLE_Q_ROWS, n_heads, SB_HEAD_DIM)
    qbd = jnp.einsum("bthd,hg->bhtgd", q8, jnp.eye(n_heads, dtype=BF16)).reshape(b, nrow, hd_all)
    brow = jnp.broadcast_to(jnp.repeat(bias.astype(F32), SAMPLE_Q_ROWS)[:, None], (nrow, page))
    pad = ((0, 0), (0, page - k_new.shape[1]), (0, 0))
    kn = jnp.pad(k_new, pad)
    vn = jnp.pad(v_new, pad)

    def cache_map(bi, p, pt):
        return (pt[bi, n_pages - jnp.maximum(p, 1)], 0, 0)

    grid_spec = pltpu.PrefetchScalarGridSpec(
        num_scalar_prefetch=1,
        grid=(b, n_pages + 1),
        in_specs=[pl.BlockSpec((1, nrow, hd_all), lambda bi, p, pt: (bi, 0, 0)),
                  pl.BlockSpec((nrow, page), lambda bi, p, pt: (0, 0)),
                  pl.BlockSpec((1, page, hd_all), lambda bi, p, pt: (bi, 0, 0)),
                  pl.BlockSpec((1, page, hd_all), lambda bi, p, pt: (bi, 0, 0)),
                  pl.BlockSpec((1, page, hd_all), cache_map),
                  pl.BlockSpec((1, page, hd_all), cache_map),
                  pl.BlockSpec((page, page), lambda bi, p, pt: (0, 0))],
        out_specs=pl.BlockSpec((1, SAMPLE_Q_ROWS, hd_all), lambda bi, p, pt: (bi, 0, 0)),
        scratch_shapes=[pltpu.VMEM((nrow, hd_all), F32), pltpu.VMEM((nrow, 1), F32)])
    return pl.pallas_call(
        functools.partial(_sb_sample_kernel, n_heads=n_heads),
        grid_spec=grid_spec,
        out_shape=jax.ShapeDtypeStruct((b, SAMPLE_Q_ROWS, hd_all), F32),
        compiler_params=_params("arbitrary", "arbitrary"), name="sb_sample",
    )(page_table, qbd, brow, kn, vn, cache_k, cache_v, _tri(page))


def _ret_kernel(dc_ref, q_ref, k_ref, v_ref, g_ref, s0_ref, din_ref, dq_ref, dk_ref, y_ref, so_ref, st_sc):
    h = pl.program_id(1)
    c = pl.program_id(2)

    @pl.when(c == 0)
    def _():
        st_sc[...] = s0_ref[0, 0]

    q = q_ref[0]
    k = k_ref[0]
    v = v_ref[0]
    state = st_sc[...]
    att = _dot_nt(q, k) * din_ref[0]
    o = _dot(att.astype(BF16), v)
    cross = _dot(q, state.astype(BF16))
    dq = dq_ref[0]
    dk = dk_ref[0]
    lanes = dq.shape[1]
    o = o + cross * jnp.concatenate([dq] * (RET_DV // lanes), axis=1)
    kd = (k.astype(F32) * jnp.concatenate([dk] * (RET_DK // lanes), axis=1)).astype(BF16)
    st_sc[...] = state * dc_ref[h] + _dot_tn(kd, v)
    o = o * lax.rsqrt(jnp.mean(o * o, axis=-1, keepdims=True) + NORM_EPS)
    y_ref[0] = (g_ref[0].astype(F32) * o).astype(BF16)

    @pl.when(c == pl.num_programs(2) - 1)
    def _():
        so_ref[0, 0] = st_sc[...]


def _retention(q, k, v, g, s0, chunk, c_true):
    b, s, _ = q.shape
    assert s % chunk == 0
    lg = jnp.log1p(-(2.0 ** (-5.0 - jnp.arange(RET_HEADS, dtype=F32))))
    idx = jnp.arange(chunk, dtype=F32)
    diff = idx[:, None] - idx[None, :]
    din = jnp.exp(jnp.where(diff >= 0, diff[None] * lg[:, None, None], -jnp.inf))
    dq = jnp.exp((idx + 1.0)[None, :] * lg[:, None])
    dk = jnp.where(idx[None, :] < c_true, jnp.exp((c_true - 1.0 - idx)[None, :] * lg[:, None]), 0.0)
    dc = jnp.exp(c_true * lg)
    lanes = 128
    dq = jnp.broadcast_to(dq[:, :, None], (RET_HEADS, chunk, lanes))
    dk = jnp.broadcast_to(dk[:, :, None], (RET_HEADS, chunk, lanes))
    return pl.pallas_call(
        _ret_kernel,
        grid=(b, RET_HEADS, s // chunk),
        in_specs=[pl.BlockSpec(memory_space=pltpu.SMEM),
                  pl.BlockSpec((1, chunk, RET_DK), lambda bi, h, c: (bi, c, h)),
                  pl.BlockSpec((1, chunk, RET_DK), lambda bi, h, c: (bi, c, h)),
                  pl.BlockSpec((1, chunk, RET_DV), lambda bi, h, c: (bi, c, h)),
                  pl.BlockSpec((1, chunk, RET_DV), lambda bi, h, c: (bi, c, h)),
                  pl.BlockSpec((1, 1, RET_DK, RET_DV), lambda bi, h, c: (bi, h, 0, 0)),
                  pl.BlockSpec((1, chunk, chunk), lambda bi, h, c: (h, 0, 0)),
                  pl.BlockSpec((1, chunk, lanes), lambda bi, h, c: (h, 0, 0)),
                  pl.BlockSpec((1, chunk, lanes), lambda bi, h, c: (h, 0, 0))],
        out_specs=[pl.BlockSpec((1, chunk, RET_DV), lambda bi, h, c: (bi, c, h)),
                   pl.BlockSpec((1, 1, RET_DK, RET_DV), lambda bi, h, c: (bi, h, 0, 0))],
        out_shape=[jax.ShapeDtypeStruct((b, s, RET_HEADS * RET_DV), BF16),
                   jax.ShapeDtypeStruct(s0.shape, F32)],
        scratch_shapes=[pltpu.VMEM((RET_DK, RET_DV), F32)],
        compiler_params=_params("arbitrary", "arbitrary", "arbitrary"), name="retention",
    )(dc, q, k, v, g, s0, din, dq, dk)


def _conv_kernel(x_ref, g_ref, wb_ref, wc_ref, wh_ref, cw_ref, buf_ref, y_ref, u_ref, ue_sc, *, ts):
    s = pl.program_id(2)

    @pl.when(s == 0)
    def _():
        ue_sc[0:CONV_HALO, :] = buf_ref[0]

    xn = _rms(x_ref[0], g_ref[...]).astype(BF16)
    bb = _dot(xn, wb_ref[...])
    u = _dot(xn, wc_ref[...]) * _dot(xn, wh_ref[...])
    u_ref[0] = u
    ue_sc[CONV_HALO:CONV_HALO + ts, :] = u
    ue = ue_sc[...]
    conv = cw_ref[CONV_WIDTH - 1:CONV_WIDTH, :] * u
    for t in range(CONV_WIDTH - 1):
        shift = CONV_WIDTH - 1 - t
        conv = conv + cw_ref[t:t + 1, :] * pltpu.roll(ue, shift, 0)[CONV_HALO:, :]
    y_ref[0] = (bb * conv).astype(BF16)
    ue_sc[0:CONV_HALO, :] = ue_sc[ts:ts + CONV_HALO, :]


def _short_conv(x, gain, w_in, conv_w, buf, ts, tn=512):
    b, s, d = x.shape
    ts = min(ts, s)
    assert s % ts == 0 and d % tn == 0
    nb = d // tn
    return pl.pallas_call(
        functools.partial(_conv_kernel, ts=ts),
        grid=(nb, b, s // ts),
        in_specs=[pl.BlockSpec((1, ts, d), lambda n, bi, si: (bi, si, 0)),
                  pl.BlockSpec((1, d), lambda n, bi, si: (0, 0)),
                  pl.BlockSpec((d, tn), lambda n, bi, si: (0, n)),
                  pl.BlockSpec((d, tn), lambda n, bi, si: (0, n + nb)),
                  pl.BlockSpec((d, tn), lambda n, bi, si: (0, n + 2 * nb)),
                  pl.BlockSpec((CONV_WIDTH, tn), lambda n, bi, si: (0, n)),
                  pl.BlockSpec((1, CONV_HALO, tn), lambda n, bi, si: (bi, 0, n))],
        out_specs=[pl.BlockSpec((1, ts, tn), lambda n, bi, si: (bi, si, n)),
                   pl.BlockSpec((1, ts, tn), lambda n, bi, si: (bi, si, n))],
        out_shape=[jax.ShapeDtypeStruct((b, s, d), BF16), jax.ShapeDtypeStruct((b, s, d), F32)],
        scratch_shapes=[pltpu.VMEM((ts + CONV_HALO, tn), F32)],
        compiler_params=_params("arbitrary", "arbitrary", "arbitrary"), name="short_conv",
    )(x, gain.reshape(1, d), w_in, w_in, w_in, conv_w, buf)


def _rope_tables(pos):
    inv = ROPE_BASE ** (-jnp.arange(0, RET_DK, 2, dtype=F32) / RET_DK)
    ang = pos.astype(F32)[:, None] * inv[None, :]
    return jnp.cos(ang), jnp.sin(ang)


def _trunk(x, pos0, n_valid, pool_buf, ret_s0, conv_buf, paged, wts, ts):
    b, s, d = x.shape
    m = b * s
    depth = wts["ffn_norm"].shape[0]
    d_ff = wts["ffn_w2"].shape[2]
    pools, ks, vs, rets, convs = [], [], [], [], []
    x = x.reshape(m, d)
    for i in range(depth):
        j, mixer = i // 4, i % 4
        x = _ffn(x, wts["ffn_norm"][i, 0], wts["ffn_w13"][i, 0], wts["ffn_w2"][i, 0], d_ff)
        gain = wts["mix_norm"][i]
        if mixer == 0:
            buf = jnp.pad(pool_buf[j], ((0, 0), (POOL_HALO - pool_buf.shape[2], 0), (0, 0)))
            x3, xn = _pool_mixer(x.reshape(b, s, d), gain, buf, wts["pool_w"][j], wts["pool_scale"][j], pos0, ts)
            x = x3.reshape(m, d)
            pools.append(jnp.concatenate([pool_buf[j], xn[:, :n_valid]], axis=1)[:, -pool_buf.shape[2]:])
        elif mixer == 1:
            w = wts["sb_w_qkv"][j]
            hd_all = SB_HEADS * SB_HEAD_DIM
            gspec = pl.BlockSpec((1, SB_HEAD_DIM), lambda i_, j_: (0, 0))
            n_h = 512 // SB_HEAD_DIM
            q = _norm_mm(x, gain, w, (0,), hd_all,
                         functools.partial(_ep_sb_q, n_heads=n_h, scale=1.0 / math.sqrt(SB_HEAD_DIM)), (BF16,),
                         extras=[(wts["sb_q_norm"][j].reshape(1, -1), gspec)], name="sb_q")[0]
            k32, k16 = _norm_mm(x, gain, w, (hd_all,), hd_all, functools.partial(_ep_sb_k, n_heads=n_h),
                                (F32, BF16), extras=[(wts["sb_k_norm"][j].reshape(1, -1), gspec)], name="sb_k")
            v32, v16 = _norm_mm(x, gain, w, (2 * hd_all,), hd_all, _ep_copy2, (F32, BF16), name="sb_v")
            ks.append(k32.reshape(b, s, SB_HEADS, SB_HEAD_DIM)[:, :n_valid])
            vs.append(v32.reshape(b, s, SB_HEADS, SB_HEAD_DIM)[:, :n_valid])
            if paged:
                ck = wts["cache_sb_k"][j]
                cv = wts["cache_sb_v"][j]
                ck = ck.reshape(ck.shape[0], ck.shape[1], hd_all)
                cv = cv.reshape(cv.shape[0], cv.shape[1], hd_all)
                o = _sb_sample(q.reshape(b, s, hd_all), k32.reshape(b, s, hd_all), v32.reshape(b, s, hd_all),
                               ck, cv, wts["page_table"], wts["sb_bias"][j])
                o = jnp.pad(o.astype(BF16), ((0, 0), (0, s - SAMPLE_Q_ROWS), (0, 0)))
            else:
                o = _sb_prompt(q.reshape(b, s, hd_all), k16.reshape(b, s, hd_all), v16.reshape(b, s, hd_all),
                               wts["sb_bias"][j].astype(F32))
            x = _mm_res(o.reshape(m, hd_all), wts["sb_w_o"][j], x, 1.0, name="sb_out")
        elif mixer == 2:
            w = wts["ret_w_qkvg"][j]
            hk, hv = RET_HEADS * RET_DK, RET_HEADS * RET_DV
            pos = pos0 + jnp.arange(s, dtype=jnp.int32)
            cos, sin = _rope_tables(jnp.tile(pos, b))
            tspec = pl.BlockSpec((min(1024, m), RET_DK // 2), lambda i_, j_: (i_, 0))
            rot = lambda sc: functools.partial(_ep_rotary, n_heads=512 // RET_DK, scale=sc)
            q = _norm_mm(x, gain, w, (0,), hk, rot(1.0), (BF16,), extras=[(cos, tspec), (sin, tspec)],
                         name="ret_q")[0]
            k = _norm_mm(x, gain, w, (hk,), hk, rot(RET_DK ** -0.5), (BF16,),
                         extras=[(cos, tspec), (sin, tspec)], name="ret_k")[0]
            v = _norm_mm(x, gain, w, (2 * hk,), hv, _ep_bf16, (BF16,), name="ret_v")[0]
            g = _norm_mm(x, gain, w, (2 * hk + hv,), hv, _ep_silu, (BF16,), name="ret_g")[0]
            chunk = min(256, s)
            c_true = min(chunk, n_valid)
            y, s_new = _retention(q.reshape(b, s, hk), k.reshape(b, s, hk), v.reshape(b, s, hv),
                                  g.reshape(b, s, hv), ret_s0[j].astype(F32), chunk, c_true)
            rets.append(s_new)
            x = _mm_res(y.reshape(m, hv), wts["ret_w_o"][j], x, 1.0, name="ret_out")
        else:
            cb = conv_buf[j]
            buf = jnp.pad(cb, ((0, 0), (CONV_HALO - cb.shape[1], 0), (0, 0)))
            yb, u = _short_conv(x.reshape(b, s, d), gain, wts["conv_w_in"][j], wts["conv_w"][j], buf, ts)
            convs.append(jnp.concatenate([cb, u[:, :n_valid]], axis=1)[:, -cb.shape[1]:])
            x = _mm_res(yb.reshape(m, d), wts["conv_w_out"][j], x, 1.0, name="conv_out")
        x = _ffn(x, wts["ffn_norm"][i, 1], wts["ffn_w13"][i, 1], wts["ffn_w2"][i, 1], d_ff)
    x = x.reshape(b, s, d)[:, :n_valid]
    return x, jnp.stack(pools), jnp.stack(ks), jnp.stack(vs), jnp.stack(rets), jnp.stack(convs)


def kernel(x_prompt, x_sample, state_pool, cache_sb_k, cache_sb_v, page_table, state_ret, state_conv,
           ffn_norm, ffn_w13, ffn_w2, mix_norm, pool_w, pool_scale, sb_w_qkv, sb_q_norm, sb_k_norm,
           sb_bias, sb_w_o, ret_w_qkvg, ret_w_o, conv_w_in, conv_w, conv_w_out):
    wts = dict(
        ffn_norm=ffn_norm, ffn_w13=ffn_w13.astype(BF16), ffn_w2=ffn_w2.astype(BF16), mix_norm=mix_norm,
        pool_w=pool_w.astype(BF16), pool_scale=pool_scale, sb_w_qkv=sb_w_qkv.astype(BF16),
        sb_q_norm=sb_q_norm, sb_k_norm=sb_k_norm, sb_bias=sb_bias, sb_w_o=sb_w_o.astype(BF16),
        ret_w_qkvg=ret_w_qkvg.astype(BF16), ret_w_o=ret_w_o.astype(BF16),
        conv_w_in=conv_w_in.astype(BF16), conv_w=conv_w, conv_w_out=conv_w_out.astype(BF16),
        cache_sb_k=cache_sb_k, cache_sb_v=cache_sb_v, page_table=page_table)

    bp, sp, d = x_prompt.shape
    y_prompt, pool_p, k_p, v_p, ret_p, conv_p = _trunk(
        x_prompt, 0, sp,
        jnp.zeros((state_pool.shape[0], bp) + state_pool.shape[2:], F32),
        jnp.zeros((state_ret.shape[0], bp) + state_ret.shape[2:], F32),
        jnp.zeros((state_conv.shape[0], bp) + state_conv.shape[2:], F32),
        False, wts, 256)

    bs, ss, _ = x_sample.shape
    past_len = page_table.shape[1] * cache_sb_k.shape[2]
    xs = jnp.pad(x_sample, ((0, 0), (0, SAMPLE_ROWS - ss), (0, 0)))
    y_sample, pool_s, k_s, v_s, ret_s, conv_s = _trunk(
        xs, past_len, ss, state_pool, state_ret, state_conv, True, wts, SAMPLE_ROWS)
    return (y_prompt, y_sample, pool_p, pool_s, k_p, v_p, k_s, v_s, ret_p, ret_s, conv_p, conv_s)
```

```python
import functools
import math

import jax
import jax.numpy as jnp
from jax import lax
from jax.experimental import pallas as pl
from jax.experimental.pallas import tpu as pltpu

F32 = jnp.float32
BF16 = jnp.bfloat16

NORM_EPS = 1e-6
POOL_WINDOWS = (2, 4, 8, 16)
POOL_HALO = 16
SB_HEADS = 16
SB_HEAD_DIM = 128
RET_HEADS = 8
RET_DK = 256
RET_DV = 512
ROPE_BASE = 10000.0
LOG2_E = 1.4426950408889634
CONV_WIDTH = 3
CONV_HALO = 8
SAMPLE_ROWS = 16

VMEM_LIMIT_BYTES = 56 * 1024 * 1024


def _params(*sem):
    return pltpu.CompilerParams(dimension_semantics=sem, vmem_limit_bytes=VMEM_LIMIT_BYTES)


def _rms(x, g):
    ms = jnp.mean(x * x, axis=-1, keepdims=True)
    return x * lax.rsqrt(ms + NORM_EPS) * g


def _dot(a, b):
    return jnp.dot(a, b, preferred_element_type=F32)


def _dot_nt(a, b):
    return lax.dot_general(a, b, (((1,), (1,)), ((), ())), preferred_element_type=F32)


def _dot_tn(a, b):
    return lax.dot_general(a, b, (((0,), (0,)), ((), ())), preferred_element_type=F32)


def _norm_mm_kernel(*refs, n_w, n_extra, epilogue):
    x_ref, g_ref = refs[0], refs[1]
    w_refs = refs[2:2 + n_w]
    extra = refs[2 + n_w:2 + n_w + n_extra]
    outs = refs[2 + n_w + n_extra:-1]
    xn_sc = refs[-1]

    @pl.when(pl.program_id(1) == 0)
    def _():
        xn_sc[...] = _rms(x_ref[...], g_ref[...]).astype(BF16)

    xn = xn_sc[...]
    accs = [_dot(xn, w[...]) for w in w_refs]
    epilogue(accs, extra, outs)


def _w_spec(w, lead, k, tn, col_block):
    assert w.ndim == len(lead) + 2 and w.shape[-2] == k
    return pl.BlockSpec((None,) * len(lead) + (k, tn), lambda i, j: tuple(lead) + (0, col_block(i, j)))


def _norm_mm(x, gain, w, lead, col_offsets, n_cols, epilogue, out_dtypes, extras=(), tm=1024, tn=512,
             name="norm_mm"):
    m, k = x.shape
    tm = min(tm, m)
    assert m % tm == 0 and n_cols % tn == 0 and all(off % tn == 0 for off in col_offsets)
    grid = (m // tm, n_cols // tn)
    in_specs = [pl.BlockSpec((tm, k), lambda i, j: (i, 0)),
                pl.BlockSpec((1, k), lambda i, j: (0, 0))]
    args = [x, gain.reshape(1, k)]
    for off in col_offsets:
        in_specs.append(_w_spec(w, lead, k, tn, functools.partial(lambda i, j, o: j + o, o=off // tn)))
        args.append(w)
    for arr, spec in extras:
        in_specs.append(spec)
        args.append(arr)
    out_shape = [jax.ShapeDtypeStruct((m, n_cols), dt) for dt in out_dtypes]
    out_specs = [pl.BlockSpec((tm, tn), lambda i, j: (i, j)) for _ in out_dtypes]
    kern = functools.partial(_norm_mm_kernel, n_w=len(col_offsets), n_extra=len(extras),
                             epilogue=epilogue)
    return pl.pallas_call(
        kern, grid=grid, in_specs=in_specs, out_specs=out_specs, out_shape=out_shape,
        scratch_shapes=[pltpu.VMEM((tm, k), BF16)],
        compiler_params=_params("arbitrary", "arbitrary"), name=name)(*args)


def _norm_proj_kernel(*refs, n_extra, segs):
    x_ref, g_ref, w_ref = refs[:3]
    extra = refs[3:3 + n_extra]
    outs = refs[3 + n_extra:-1]
    xn_sc = refs[-1]
    j = pl.program_id(1)

    @pl.when(j == 0)
    def _():
        xn_sc[...] = _rms(x_ref[...], g_ref[...]).astype(BF16)

    acc = _dot(xn_sc[...], w_ref[...])
    first = 0
    for start, n_tiles, epilogue, out_dtypes in segs:
        seg_outs = outs[first:first + len(out_dtypes)]
        first += len(out_dtypes)

        @pl.when((j >= start) & (j < start + n_tiles))
        def _(epilogue=epilogue, seg_outs=seg_outs):
            epilogue([acc], extra, seg_outs)


def _norm_proj(x, gain, w, lead, segs, extras=(), tm=1024, tn=512, name="norm_proj"):
    m, k = x.shape
    tm = min(tm, m)
    assert m % tm == 0 and all(n % tn == 0 for n, _, _ in segs)
    in_specs = [pl.BlockSpec((tm, k), lambda i, j: (i, 0)),
                pl.BlockSpec((1, k), lambda i, j: (0, 0)),
                _w_spec(w, lead, k, tn, lambda i, j: j)]
    args = [x, gain.reshape(1, k), w]
    for arr, spec in extras:
        in_specs.append(spec)
        args.append(arr)
    ksegs, out_shape, out_specs, start = [], [], [], 0
    for n_cols, epilogue, out_dtypes in segs:
        n_tiles = n_cols // tn
        ksegs.append((start, n_tiles, epilogue, out_dtypes))
        for dt in out_dtypes:
            out_shape.append(jax.ShapeDtypeStruct((m, n_cols), dt))
            out_specs.append(pl.BlockSpec(
                (tm, tn), functools.partial(lambda i, j, s0, nt: (i, jnp.clip(j - s0, 0, nt - 1)), s0=start, nt=n_tiles)))
        start += n_tiles
    return pl.pallas_call(
        functools.partial(_norm_proj_kernel, n_extra=len(extras), segs=tuple(ksegs)),
        grid=(m // tm, start), in_specs=in_specs, out_specs=out_specs, out_shape=out_shape,
        scratch_shapes=[pltpu.VMEM((tm, k), BF16)],
        compiler_params=_params("arbitrary", "arbitrary"), name=name)(*args)


def _ep_swiglu(accs, extra, outs):
    a, b = accs
    outs[0][...] = (a * jax.nn.sigmoid(a) * b).astype(BF16)


def _ep_sb_q(accs, extra, outs, *, n_heads, scale):
    g = extra[0][...]
    for h in range(n_heads):
        sl = slice(h * SB_HEAD_DIM, (h + 1) * SB_HEAD_DIM)
        outs[0][:, sl] = (_rms(accs[0][:, sl], g) * scale).astype(BF16)


def _ep_sb_k(accs, extra, outs, *, n_heads):
    g = extra[1][...]
    for h in range(n_heads):
        sl = slice(h * SB_HEAD_DIM, (h + 1) * SB_HEAD_DIM)
        kn = _rms(accs[0][:, sl], g)
        outs[0][:, sl] = kn
        outs[1][:, sl] = kn.astype(BF16)


def _ep_copy2(accs, extra, outs):
    outs[0][...] = accs[0]
    outs[1][...] = accs[0].astype(BF16)


def _ep_bf16(accs, extra, outs):
    outs[0][...] = accs[0].astype(BF16)


def _ep_silu(accs, extra, outs):
    a = accs[0]
    outs[0][...] = (a * jax.nn.sigmoid(a)).astype(BF16)


def _ep_rotary(accs, extra, outs, *, n_heads, scale):
    cos = extra[0][...]
    sin = extra[1][...]
    half = RET_DK // 2
    for h in range(n_heads):
        x1 = accs[0][:, h * RET_DK:h * RET_DK + half]
        x2 = accs[0][:, h * RET_DK + half:(h + 1) * RET_DK]
        outs[0][:, h * RET_DK:h * RET_DK + half] = ((x1 * cos - x2 * sin) * scale).astype(BF16)
        outs[0][:, h * RET_DK + half:(h + 1) * RET_DK] = ((x1 * sin + x2 * cos) * scale).astype(BF16)


def _mm_res_kernel(a_ref, w_ref, r_ref, o_ref, *, scale):
    o_ref[...] = r_ref[...] + scale * _dot(a_ref[...], w_ref[...])


def _mm_res(a, w, lead, res, scale, tm=1024, tn=512, name="mm_res"):
    m, k = a.shape
    n = w.shape[-1]
    tm = min(tm, m)
    assert m % tm == 0 and n % tn == 0
    return pl.pallas_call(
        functools.partial(_mm_res_kernel, scale=scale),
        grid=(m // tm, n // tn),
        in_specs=[pl.BlockSpec((tm, k), lambda i, j: (i, 0)),
                  _w_spec(w, lead, k, tn, lambda i, j: j),
                  pl.BlockSpec((tm, tn), lambda i, j: (i, j))],
        out_specs=pl.BlockSpec((tm, tn), lambda i, j: (i, j)),
        out_shape=jax.ShapeDtypeStruct((m, n), F32),
        compiler_params=_params("arbitrary", "arbitrary"), name=name)(a, w, res)


def _ffn(x, gain, w13, w2, lead):
    d_ff = w2.shape[-2]
    h = _norm_mm(x, gain, w13, lead, (0, d_ff), d_ff, _ep_swiglu, (BF16,), name="ffn_up")[0]
    return _mm_res(h, w2, lead, x, 0.5, name="ffn_down")


def _pool_kernel(x_ref, g_ref, buf_ref, w_ref, sc_ref, o_ref, xn_ref, xe_sc, *, ts, pos0):
    s = pl.program_id(1)

    @pl.when(s == 0)
    def _():
        xe_sc[0:POOL_HALO, :] = buf_ref[0]

    x = x_ref[0]
    xn = _rms(x, g_ref[...])
    xn_ref[0] = xn
    xe_sc[POOL_HALO:POOL_HALO + ts, :] = xn

    gd = x.shape[1] // len(POOL_WINDOWS)
    pos = pos0 + s * ts + lax.broadcasted_iota(jnp.int32, (ts, 1), 0)
    for gi, win in enumerate(POOL_WINDOWS):
        sl = slice(gi * gd, (gi + 1) * gd)
        acc = xe_sc[:, sl]
        step = 1
        while step < win:
            acc = acc + pltpu.roll(acc, step, 0)
            step *= 2
        cnt = jnp.minimum(win, pos + 1).astype(F32)
        pooled = acc[POOL_HALO:, :] / cnt
        d = (pooled - xn[:, sl]).astype(BF16)
        y = _dot(d, w_ref[gi])
        o_ref[0, :, sl] = x[:, sl] + y * sc_ref[:, sl]

    xe_sc[0:POOL_HALO, :] = xe_sc[ts:ts + POOL_HALO, :]


def _pool_mixer(x, gain, buf, w, scale, pos0, ts):
    b, s, d = x.shape
    ts = min(ts, s)
    assert s % ts == 0
    gd = d // len(POOL_WINDOWS)
    return pl.pallas_call(
        functools.partial(_pool_kernel, ts=ts, pos0=pos0),
        grid=(b, s // ts),
        in_specs=[pl.BlockSpec((1, ts, d), lambda i, j: (i, j, 0)),
                  pl.BlockSpec((1, d), lambda i, j: (0, 0)),
                  pl.BlockSpec((1, POOL_HALO, d), lambda i, j: (i, 0, 0)),
                  pl.BlockSpec((len(POOL_WINDOWS), gd, gd), lambda i, j: (0, 0, 0)),
                  pl.BlockSpec((1, d), lambda i, j: (0, 0))],
        out_specs=[pl.BlockSpec((1, ts, d), lambda i, j: (i, j, 0)),
                   pl.BlockSpec((1, ts, d), lambda i, j: (i, j, 0))],
        out_shape=[jax.ShapeDtypeStruct((b, s, d), F32), jax.ShapeDtypeStruct((b, s, d), F32)],
        scratch_shapes=[pltpu.VMEM((ts + POOL_HALO, d), F32)],
        compiler_params=_params("arbitrary", "arbitrary"), name="pool_mixer",
    )(x, gain.reshape(1, d), buf, w, scale.reshape(1, d))


def _sb_softplus2(z, mask):
    sp = jnp.maximum(z, 0.0) + jnp.log2(1.0 + jnp.exp2(-jnp.abs(z)))
    return sp if mask is None else jnp.where(mask, sp, 0.0)


def _sb_neg_cumsum(sp, tri_neg):
    hi = sp.astype(BF16)
    lo = (sp - hi.astype(F32)).astype(BF16)
    return _dot(jnp.concatenate([hi, lo], axis=1), tri_neg)


def _sb_weights(z, tri_neg, carry, mask):
    sp = _sb_softplus2(z, mask)
    a = jnp.exp2(z + _sb_neg_cumsum(sp, tri_neg) + carry)
    if mask is not None:
        a = jnp.where(mask, a, 0.0)
    return a, carry - jnp.sum(sp, axis=-1, keepdims=True)


def _sb_prompt_kernel(bias_ref, q_ref, k_ref, v_ref, tri_ref, o_ref, *, tq, heads):
    g = pl.program_id(1)
    i = pl.program_id(2)
    tri = tri_ref[...]
    rows = lax.broadcasted_iota(jnp.int32, (tq, tq), 0)
    cols = lax.broadcasted_iota(jnp.int32, (tq, tq), 1)
    hd = SB_HEAD_DIM

    def block(off, state, mask):
        sls = [slice(hh * hd, (hh + 1) * hd) for hh in range(heads)]
        zs = [_dot_nt(q_ref[0, :, sl], k_ref[0, pl.ds(off, tq), sl]) + bias_ref[g * heads + hh] * LOG2_E
              for hh, sl in enumerate(sls)]
        sps = [_sb_softplus2(z, mask) for z in zs]
        css = [_sb_neg_cumsum(sp, tri) for sp in sps]
        out = []
        for hh, sl in enumerate(sls):
            carry, acc = state[hh]
            a = jnp.exp2(zs[hh] + css[hh] + carry)
            if mask is not None:
                a = jnp.where(mask, a, 0.0)
            out.append((carry - jnp.sum(sps[hh], axis=-1, keepdims=True),
                        acc + _dot(a.astype(BF16), v_ref[0, pl.ds(off, tq), sl])))
        return tuple(out)

    state = tuple((jnp.zeros((tq, 1), F32), jnp.zeros((tq, hd), F32)) for _ in range(heads))
    state = block(pl.multiple_of(i * tq, tq), state, cols < rows)
    state = lax.fori_loop(0, i, lambda n, st: block(pl.multiple_of((i - 1 - n) * tq, tq), st, None), state)
    for hh in range(heads):
        o_ref[0, :, hh * hd:(hh + 1) * hd] = state[hh][1].astype(BF16)


def _tri_neg(n):
    r = lax.broadcasted_iota(jnp.int32, (2 * n, n), 0) % n
    c = lax.broadcasted_iota(jnp.int32, (2 * n, n), 1)
    return jnp.where(r >= c, -1.0, 0.0).astype(BF16)


def _sb_prompt(q, k, v, bias, tq=256, heads=4):
    b, s, hd_all = q.shape
    tq = min(tq, s)
    assert s % tq == 0
    n_heads = hd_all // SB_HEAD_DIM
    w = heads * SB_HEAD_DIM
    return pl.pallas_call(
        functools.partial(_sb_prompt_kernel, tq=tq, heads=heads),
        grid=(b, n_heads // heads, s // tq),
        in_specs=[pl.BlockSpec(memory_space=pltpu.SMEM),
                  pl.BlockSpec((1, tq, w), lambda bi, h, i: (bi, i, h)),
                  pl.BlockSpec((1, s, w), lambda bi, h, i: (bi, 0, h)),
                  pl.BlockSpec((1, s, w), lambda bi, h, i: (bi, 0, h)),
                  pl.BlockSpec((2 * tq, tq), lambda bi, h, i: (0, 0))],
        out_specs=pl.BlockSpec((1, tq, w), lambda bi, h, i: (bi, i, h)),
        out_shape=jax.ShapeDtypeStruct((b, s, hd_all), BF16),
        compiler_params=_params("arbitrary", "arbitrary", "arbitrary"), name="sb_prompt",
    )(bias, q, k, v, _tri_neg(tq))


def _sb_sample_kernel(pt_ref, q_ref, brow_ref, kn_ref, vn_ref, *rest, n_heads, pps):
    kc_refs, vc_refs = rest[:pps], rest[pps:2 * pps]
    tri_ref, o_ref, acc_sc, carry_sc = rest[2 * pps:]
    p = pl.program_id(1)
    rows, page = brow_ref.shape
    rq = rows // n_heads
    hd = SB_HEAD_DIM

    def visit(k_ref, v_ref, mask):
        head = lambda ref, h: ref[0, pl.ds(h, page, stride=n_heads), :].astype(BF16)
        zs = [_dot_nt(q_ref[0, :, h * hd:(h + 1) * hd], head(k_ref, h)) for h in range(n_heads)]
        z = jnp.concatenate(zs, axis=0) + brow_ref[...]
        a, carry = _sb_weights(z, tri_ref[...], carry_sc[...], mask)
        carry_sc[...] = carry
        a = a.astype(BF16)
        for h in range(n_heads):
            acc_sc[:, h * hd:(h + 1) * hd] += _dot(a[h * rq:(h + 1) * rq, :], head(v_ref, h))

    @pl.when(p == 0)
    def _():
        acc_sc[...] = jnp.zeros_like(acc_sc)
        carry_sc[...] = jnp.zeros_like(carry_sc)
        t = lax.broadcasted_iota(jnp.int32, (rows, page), 0) % rq
        sk = lax.broadcasted_iota(jnp.int32, (rows, page), 1)
        visit(kn_ref, vn_ref, sk < t)

    @pl.when(p > 0)
    def _():
        for u in range(pps):
            visit(kc_refs[u], vc_refs[u], None)

    @pl.when(p == pl.num_programs(1) - 1)
    def _():
        o_ref[0] = acc_sc[...].astype(BF16)


def _sb_sample(q, k_new, v_new, cache_k, cache_v, page_table, bias, pps=4):
    b, r, hd_all = q.shape
    n_pages = page_table.shape[1]
    _, page, n_heads, hd = cache_k.shape
    assert n_pages % pps == 0
    rows = n_heads * r
    brow = jnp.broadcast_to(jnp.repeat(bias.astype(F32) * LOG2_E, r)[:, None], (rows, page))
    pad = ((0, 0), (0, page - r), (0, 0), (0, 0))
    flat = lambda c: c.reshape(c.shape[0], page * n_heads, hd)

    def cache_map(u, bi, p, pt):
        return (pt[bi, n_pages - 1 - ((jnp.maximum(p, 1) - 1) * pps + u)], 0, 0)

    blk = (1, page * n_heads, hd)
    cache_specs = [pl.BlockSpec(blk, functools.partial(cache_map, u)) for u in range(pps)]
    grid_spec = pltpu.PrefetchScalarGridSpec(
        num_scalar_prefetch=1,
        grid=(b, n_pages // pps + 1),
        in_specs=[pl.BlockSpec((1, r, hd_all), lambda bi, p, pt: (bi, 0, 0)),
                  pl.BlockSpec((rows, page), lambda bi, p, pt: (0, 0)),
                  pl.BlockSpec(blk, lambda bi, p, pt: (bi, 0, 0)),
                  pl.BlockSpec(blk, lambda bi, p, pt: (bi, 0, 0))] + cache_specs + cache_specs + [
                  pl.BlockSpec((2 * page, page), lambda bi, p, pt: (0, 0))],
        out_specs=pl.BlockSpec((1, r, hd_all), lambda bi, p, pt: (bi, 0, 0)),
        scratch_shapes=[pltpu.VMEM((r, hd_all), F32), pltpu.VMEM((rows, 1), F32)])
    ck, cv = flat(cache_k), flat(cache_v)
    return pl.pallas_call(
        functools.partial(_sb_sample_kernel, n_heads=n_heads, pps=pps),
        grid_spec=grid_spec,
        out_shape=jax.ShapeDtypeStruct((b, r, hd_all), BF16),
        compiler_params=_params("arbitrary", "arbitrary"), name="sb_sample",
    )(page_table, q, brow, flat(jnp.pad(k_new, pad)), flat(jnp.pad(v_new, pad)),
      *([ck] * pps), *([cv] * pps), _tri_neg(page))


def _ret_kernel(dc_ref, q_ref, k_ref, v_ref, g_ref, s0_ref, din_ref, dq_ref, dk_ref, y_ref, so_ref, st_sc):
    h = pl.program_id(1)
    c = pl.program_id(2)

    @pl.when(c == 0)
    def _():
        st_sc[...] = s0_ref[0, 0]

    q = q_ref[0]
    k = k_ref[0]
    v = v_ref[0]
    state = st_sc[...]
    att = _dot_nt(q, k) * din_ref[0]
    o = _dot(att.astype(BF16), v)
    cross = _dot(q, state.astype(BF16))
    dq = dq_ref[0]
    dk = dk_ref[0]
    lanes = dq.shape[1]
    o = o + cross * jnp.concatenate([dq] * (RET_DV // lanes), axis=1)
    kd = (k.astype(F32) * jnp.concatenate([dk] * (RET_DK // lanes), axis=1)).astype(BF16)
    st_sc[...] = state * dc_ref[h] + _dot_tn(kd, v)
    o = o * lax.rsqrt(jnp.mean(o * o, axis=-1, keepdims=True) + NORM_EPS)
    y_ref[0] = (g_ref[0].astype(F32) * o).astype(BF16)

    @pl.when(c == pl.num_programs(2) - 1)
    def _():
        so_ref[0, 0] = st_sc[...]


def _retention(q, k, v, g, s0, chunk, c_true):
    b, s, _ = q.shape
    assert s % chunk == 0
    lg = jnp.log1p(-(2.0 ** (-5.0 - jnp.arange(RET_HEADS, dtype=F32))))
    idx = jnp.arange(chunk, dtype=F32)
    diff = idx[:, None] - idx[None, :]
    din = jnp.exp(jnp.where(diff >= 0, diff[None] * lg[:, None, None], -jnp.inf))
    dq = jnp.exp((idx + 1.0)[None, :] * lg[:, None])
    dk = jnp.where(idx[None, :] < c_true, jnp.exp((c_true - 1.0 - idx)[None, :] * lg[:, None]), 0.0)
    dc = jnp.exp(c_true * lg)
    lanes = 128
    dq = jnp.broadcast_to(dq[:, :, None], (RET_HEADS, chunk, lanes))
    dk = jnp.broadcast_to(dk[:, :, None], (RET_HEADS, chunk, lanes))
    return pl.pallas_call(
        _ret_kernel,
        grid=(b, RET_HEADS, s // chunk),
        in_specs=[pl.BlockSpec(memory_space=pltpu.SMEM),
                  pl.BlockSpec((1, chunk, RET_DK), lambda bi, h, c: (bi, c, h)),
                  pl.BlockSpec((1, chunk, RET_DK), lambda bi, h, c: (bi, c, h)),
                  pl.BlockSpec((1, chunk, RET_DV), lambda bi, h, c: (bi, c, h)),
                  pl.BlockSpec((1, chunk, RET_DV), lambda bi, h, c: (bi, c, h)),
                  pl.BlockSpec((1, 1, RET_DK, RET_DV), lambda bi, h, c: (bi, h, 0, 0)),
                  pl.BlockSpec((1, chunk, chunk), lambda bi, h, c: (h, 0, 0)),
                  pl.BlockSpec((1, chunk, lanes), lambda bi, h, c: (h, 0, 0)),
                  pl.BlockSpec((1, chunk, lanes), lambda bi, h, c: (h, 0, 0))],
        out_specs=[pl.BlockSpec((1, chunk, RET_DV), lambda bi, h, c: (bi, c, h)),
                   pl.BlockSpec((1, 1, RET_DK, RET_DV), lambda bi, h, c: (bi, h, 0, 0))],
        out_shape=[jax.ShapeDtypeStruct((b, s, RET_HEADS * RET_DV), BF16),
                   jax.ShapeDtypeStruct(s0.shape, F32)],
        scratch_shapes=[pltpu.VMEM((RET_DK, RET_DV), F32)],
        compiler_params=_params("arbitrary", "arbitrary", "arbitrary"), name="retention",
    )(dc, q, k, v, g, s0, din, dq, dk)


def _conv_kernel(x_ref, g_ref, wb_ref, wc_ref, wh_ref, cw_ref, buf_ref, y_ref, u_ref, ue_sc, *, ts):
    s = pl.program_id(2)

    @pl.when(s == 0)
    def _():
        ue_sc[0:CONV_HALO, :] = buf_ref[0]

    xn = _rms(x_ref[0], g_ref[...]).astype(BF16)
    bb = _dot(xn, wb_ref[...])
    u = _dot(xn, wc_ref[...]) * _dot(xn, wh_ref[...])
    u_ref[0] = u
    ue_sc[CONV_HALO:CONV_HALO + ts, :] = u
    ue = ue_sc[...]
    conv = cw_ref[CONV_WIDTH - 1:CONV_WIDTH, :] * u
    for t in range(CONV_WIDTH - 1):
        shift = CONV_WIDTH - 1 - t
        conv = conv + cw_ref[t:t + 1, :] * pltpu.roll(ue, shift, 0)[CONV_HALO:, :]
    y_ref[0] = (bb * conv).astype(BF16)
    ue_sc[0:CONV_HALO, :] = ue_sc[ts:ts + CONV_HALO, :]


def _short_conv(x, gain, w_in, lead, conv_w, buf, ts, tn=512):
    b, s, d = x.shape
    ts = min(ts, s)
    assert s % ts == 0 and d % tn == 0
    nb = d // tn
    wblk = (None,) * len(lead) + (d, tn)
    return pl.pallas_call(
        functools.partial(_conv_kernel, ts=ts),
        grid=(nb, b, s // ts),
        in_specs=[pl.BlockSpec((1, ts, d), lambda n, bi, si: (bi, si, 0)),
                  pl.BlockSpec((1, d), lambda n, bi, si: (0, 0)),
                  pl.BlockSpec(wblk, lambda n, bi, si: tuple(lead) + (0, n)),
                  pl.BlockSpec(wblk, lambda n, bi, si: tuple(lead) + (0, n + nb)),
                  pl.BlockSpec(wblk, lambda n, bi, si: tuple(lead) + (0, n + 2 * nb)),
                  pl.BlockSpec((CONV_WIDTH, tn), lambda n, bi, si: (0, n)),
                  pl.BlockSpec((1, CONV_HALO, tn), lambda n, bi, si: (bi, 0, n))],
        out_specs=[pl.BlockSpec((1, ts, tn), lambda n, bi, si: (bi, si, n)),
                   pl.BlockSpec((1, ts, tn), lambda n, bi, si: (bi, si, n))],
        out_shape=[jax.ShapeDtypeStruct((b, s, d), BF16), jax.ShapeDtypeStruct((b, s, d), F32)],
        scratch_shapes=[pltpu.VMEM((ts + CONV_HALO, tn), F32)],
        compiler_params=_params("arbitrary", "arbitrary", "arbitrary"), name="short_conv",
    )(x, gain.reshape(1, d), w_in, w_in, w_in, conv_w, buf)


def _rope_tables(pos):
    inv = ROPE_BASE ** (-jnp.arange(0, RET_DK, 2, dtype=F32) / RET_DK)
    ang = pos.astype(F32)[:, None] * inv[None, :]
    return jnp.cos(ang), jnp.sin(ang)


def _trunk(x, pos0, n_valid, pool_buf, ret_s0, conv_buf, paged, wts, ts):
    b, s, d = x.shape
    m = b * s
    depth = wts["ffn_norm"].shape[0]
    pools, ks, vs, rets, convs = [], [], [], [], []
    x = x.reshape(m, d)
    for i in range(depth):
        j, mixer = i // 4, i % 4
        x = _ffn(x, wts["ffn_norm"][i, 0], wts["ffn_w13"], wts["ffn_w2"], (i, 0))
        gain = wts["mix_norm"][i]
        if mixer == 0:
            buf = jnp.pad(pool_buf[j], ((0, 0), (POOL_HALO - pool_buf.shape[2], 0), (0, 0)))
            x3, xn = _pool_mixer(x.reshape(b, s, d), gain, buf, wts["pool_w"][j], wts["pool_scale"][j], pos0, ts)
            x = x3.reshape(m, d)
            pools.append(jnp.concatenate([pool_buf[j], xn[:, :n_valid]], axis=1)[:, -pool_buf.shape[2]:])
        elif mixer == 1:
            hd_all = SB_HEADS * SB_HEAD_DIM
            gspec = pl.BlockSpec((1, SB_HEAD_DIM), lambda i_, j_: (0, 0))
            n_h = 512 // SB_HEAD_DIM
            q, k32, k16, v32, v16 = _norm_proj(
                x, gain, wts["sb_w_qkv"], (j,),
                [(hd_all, functools.partial(_ep_sb_q, n_heads=n_h, scale=LOG2_E / math.sqrt(SB_HEAD_DIM)), (BF16,)),
                 (hd_all, functools.partial(_ep_sb_k, n_heads=n_h), (F32, BF16)),
                 (hd_all, _ep_copy2, (F32, BF16))],
                extras=[(wts["sb_q_norm"][j].reshape(1, -1), gspec), (wts["sb_k_norm"][j].reshape(1, -1), gspec)],
                name="sb_qkv")
            shp = (b, s, SB_HEADS, SB_HEAD_DIM)
            ks.append(k32.reshape(shp)[:, :n_valid])
            vs.append(v32.reshape(shp)[:, :n_valid])
            if paged:
                o = _sb_sample(q.reshape(b, s, hd_all), k32.reshape(shp), v32.reshape(shp),
                               wts["cache_sb_k"][j], wts["cache_sb_v"][j], wts["page_table"], wts["sb_bias"][j])
            else:
                o = _sb_prompt(q.reshape(b, s, hd_all), k16.reshape(b, s, hd_all), v16.reshape(b, s, hd_all),
                               wts["sb_bias"][j].astype(F32))
            x = _mm_res(o.reshape(m, hd_all), wts["sb_w_o"], (j,), x, 1.0, name="sb_out")
        elif mixer == 2:
            hk, hv = RET_HEADS * RET_DK, RET_HEADS * RET_DV
            pos = pos0 + jnp.arange(s, dtype=jnp.int32)
            cos, sin = _rope_tables(jnp.tile(pos, b))
            tspec = pl.BlockSpec((min(1024, m), RET_DK // 2), lambda i_, j_: (i_, 0))
            rot = lambda sc: functools.partial(_ep_rotary, n_heads=512 // RET_DK, scale=sc)
            q, k, v, g = _norm_proj(
                x, gain, wts["ret_w_qkvg"], (j,),
                [(hk, rot(1.0), (BF16,)), (hk, rot(RET_DK ** -0.5), (BF16,)), (hv, _ep_bf16, (BF16,)),
                 (hv, _ep_silu, (BF16,))],
                extras=[(cos, tspec), (sin, tspec)], name="ret_qkvg")
            chunk = min(256, s)
            c_true = min(chunk, n_valid)
            y, s_new = _retention(q.reshape(b, s, hk), k.reshape(b, s, hk), v.reshape(b, s, hv),
                                  g.reshape(b, s, hv), ret_s0[j].astype(F32), chunk, c_true)
            rets.append(s_new)
            x = _mm_res(y.reshape(m, hv), wts["ret_w_o"], (j,), x, 1.0, name="ret_out")
        else:
            cb = conv_buf[j]
            buf = jnp.pad(cb, ((0, 0), (CONV_HALO - cb.shape[1], 0), (0, 0)))
            yb, u = _short_conv(x.reshape(b, s, d), gain, wts["conv_w_in"], (j,), wts["conv_w"][j], buf, ts)
            convs.append(jnp.concatenate([cb, u[:, :n_valid]], axis=1)[:, -cb.shape[1]:])
            x = _mm_res(yb.reshape(m, d), wts["conv_w_out"], (j,), x, 1.0, name="conv_out")
        x = _ffn(x, wts["ffn_norm"][i, 1], wts["ffn_w13"], wts["ffn_w2"], (i, 1))
    x = x.reshape(b, s, d)[:, :n_valid]
    return x, jnp.stack(pools), jnp.stack(ks), jnp.stack(vs), jnp.stack(rets), jnp.stack(convs)


def kernel(x_prompt, x_sample, state_pool, cache_sb_k, cache_sb_v, page_table, state_ret, state_conv,
           ffn_norm, ffn_w13, ffn_w2, mix_norm, pool_w, pool_scale, sb_w_qkv, sb_q_norm, sb_k_norm,
           sb_bias, sb_w_o, ret_w_qkvg, ret_w_o, conv_w_in, conv_w, conv_w_out):
    wts = dict(
        ffn_norm=ffn_norm, ffn_w13=ffn_w13.astype(BF16), ffn_w2=ffn_w2.astype(BF16), mix_norm=mix_norm,
        pool_w=pool_w.astype(BF16), pool_scale=pool_scale, sb_w_qkv=sb_w_qkv.astype(BF16),
        sb_q_norm=sb_q_norm, sb_k_norm=sb_k_norm, sb_bias=sb_bias, sb_w_o=sb_w_o.astype(BF16),
        ret_w_qkvg=ret_w_qkvg.astype(BF16), ret_w_o=ret_w_o.astype(BF16),
        conv_w_in=conv_w_in.astype(BF16), conv_w=conv_w, conv_w_out=conv_w_out.astype(BF16),
        cache_sb_k=cache_sb_k, cache_sb_v=cache_sb_v, page_table=page_table)

    bp, sp, d = x_prompt.shape
    y_prompt, pool_p, k_p, v_p, ret_p, conv_p = _trunk(
        x_prompt, 0, sp,
        jnp.zeros((state_pool.shape[0], bp) + state_pool.shape[2:], F32),
        jnp.zeros((state_ret.shape[0], bp) + state_ret.shape[2:], F32),
        jnp.zeros((state_conv.shape[0], bp) + state_conv.shape[2:], F32),
        False, wts, 256)

    bs, ss, _ = x_sample.shape
    past_len = page_table.shape[1] * cache_sb_k.shape[2]
    xs = jnp.pad(x_sample, ((0, 0), (0, SAMPLE_ROWS - ss), (0, 0)))
    y_sample, pool_s, k_s, v_s, ret_s, conv_s = _trunk(
        xs, past_len, ss, state_pool, state_ret, state_conv, True, wts, SAMPLE_ROWS)
    return (y_prompt, y_sample, pool_p, pool_s, k_p, v_p, k_s, v_s, ret_p, ret_s, conv_p, conv_s)
```

```python
import functools
import math

import jax
import jax.numpy as jnp
from jax import lax
from jax.experimental import pallas as pl
from jax.experimental.pallas import tpu as pltpu

F32 = jnp.float32
BF16 = jnp.bfloat16

NORM_EPS = 1e-6
POOL_WINDOWS = (2, 4, 8, 16)
POOL_HALO = 16
SB_HEADS = 16
SB_HEAD_DIM = 128
RET_HEADS = 8
RET_DK = 256
RET_DV = 512
ROPE_BASE = 10000.0
LOG2_E = 1.4426950408889634
CONV_WIDTH = 3
CONV_HALO = 8
SAMPLE_ROWS = 16

VMEM_LIMIT_BYTES = 56 * 1024 * 1024


def _params(*sem):
    return pltpu.CompilerParams(dimension_semantics=sem, vmem_limit_bytes=VMEM_LIMIT_BYTES)


def _rms(x, g):
    ms = jnp.mean(x * x, axis=-1, keepdims=True)
    return x * lax.rsqrt(ms + NORM_EPS) * g


def _dot(a, b):
    return jnp.dot(a, b, preferred_element_type=F32)


def _dot_nt(a, b):
    return lax.dot_general(a, b, (((1,), (1,)), ((), ())), preferred_element_type=F32)


def _dot_tn(a, b):
    return lax.dot_general(a, b, (((0,), (0,)), ((), ())), preferred_element_type=F32)


def _norm_mm_kernel(*refs, n_w, n_extra, n_out, epilogue, emit):
    x_ref, g_ref = refs[0], refs[1]
    w_refs = refs[2:2 + n_w]
    extra = refs[2 + n_w:2 + n_w + n_extra]
    outs = refs[2 + n_w + n_extra:2 + n_w + n_extra + n_out]
    wb_refs = refs[2 + n_w + n_extra + n_out:-1]
    xn_sc = refs[-1]

    @pl.when(pl.program_id(1) == 0)
    def _():
        xn_sc[...] = _rms(x_ref[...], g_ref[...]).astype(BF16)

    xn = xn_sc[...]
    ws = [w[...].astype(BF16) for w in w_refs]
    if emit:
        for wb_ref, wb in zip(wb_refs, ws):
            wb_ref[...] = wb
    epilogue([_dot(xn, wb) for wb in ws], extra, outs)


def _w_spec(w, lead, k, tn, col_block):
    assert w.ndim == len(lead) + 2 and w.shape[-2] == k
    return pl.BlockSpec((None,) * len(lead) + (k, tn), lambda i, j: tuple(lead) + (0, col_block(i, j)))


def _wb_out(k, n_cols, tn):
    return jax.ShapeDtypeStruct((k, n_cols), BF16), pl.BlockSpec((k, tn), lambda i, j: (0, j))


def _norm_mm(x, gain, views, n_cols, epilogue, out_dtypes, extras=(), emit=False, tm=1024, tn=512,
             name="norm_mm"):
    m, k = x.shape
    tm = min(tm, m)
    assert m % tm == 0 and n_cols % tn == 0 and all(off % tn == 0 for _, _, off in views)
    assert not emit or m == tm
    grid = (m // tm, n_cols // tn)
    in_specs = [pl.BlockSpec((tm, k), lambda i, j: (i, 0)),
                pl.BlockSpec((1, k), lambda i, j: (0, 0))]
    args = [x, gain.reshape(1, k)]
    for w, lead, off in views:
        in_specs.append(_w_spec(w, lead, k, tn, functools.partial(lambda i, j, o: j + o, o=off // tn)))
        args.append(w)
    for arr, spec in extras:
        in_specs.append(spec)
        args.append(arr)
    out_shape = [jax.ShapeDtypeStruct((m, n_cols), dt) for dt in out_dtypes]
    out_specs = [pl.BlockSpec((tm, tn), lambda i, j: (i, j)) for _ in out_dtypes]
    if emit:
        for _ in views:
            shp, spec = _wb_out(k, n_cols, tn)
            out_shape.append(shp)
            out_specs.append(spec)
    kern = functools.partial(_norm_mm_kernel, n_w=len(views), n_extra=len(extras), n_out=len(out_dtypes),
                             epilogue=epilogue, emit=emit)
    return pl.pallas_call(
        kern, grid=grid, in_specs=in_specs, out_specs=out_specs, out_shape=out_shape,
        scratch_shapes=[pltpu.VMEM((tm, k), BF16)],
        compiler_params=_params("arbitrary", "arbitrary"), name=name)(*args)


def _norm_proj_kernel(*refs, n_extra, segs, emit):
    x_ref, g_ref, w_ref = refs[:3]
    extra = refs[3:3 + n_extra]
    outs = refs[3 + n_extra:-1]
    xn_sc = refs[-1]
    j = pl.program_id(1)

    @pl.when(j == 0)
    def _():
        xn_sc[...] = _rms(x_ref[...], g_ref[...]).astype(BF16)

    wb = w_ref[...].astype(BF16)
    if emit:
        outs[-1][...] = wb
    acc = _dot(xn_sc[...], wb)
    first = 0
    for start, n_tiles, epilogue, out_dtypes in segs:
        seg_outs = outs[first:first + len(out_dtypes)]
        first += len(out_dtypes)

        @pl.when((j >= start) & (j < start + n_tiles))
        def _(epilogue=epilogue, seg_outs=seg_outs):
            epilogue([acc], extra, seg_outs)


def _norm_proj(x, gain, w, lead, segs, extras=(), emit=False, tm=1024, tn=512, name="norm_proj"):
    m, k = x.shape
    tm = min(tm, m)
    assert m % tm == 0 and all(n % tn == 0 for n, _, _ in segs) and (not emit or m == tm)
    in_specs = [pl.BlockSpec((tm, k), lambda i, j: (i, 0)),
                pl.BlockSpec((1, k), lambda i, j: (0, 0)),
                _w_spec(w, lead, k, tn, lambda i, j: j)]
    args = [x, gain.reshape(1, k), w]
    for arr, spec in extras:
        in_specs.append(spec)
        args.append(arr)
    ksegs, out_shape, out_specs, start = [], [], [], 0
    for n_cols, epilogue, out_dtypes in segs:
        n_tiles = n_cols // tn
        ksegs.append((start, n_tiles, epilogue, out_dtypes))
        for dt in out_dtypes:
            out_shape.append(jax.ShapeDtypeStruct((m, n_cols), dt))
            out_specs.append(pl.BlockSpec(
                (tm, tn), functools.partial(lambda i, j, s0, nt: (i, jnp.clip(j - s0, 0, nt - 1)), s0=start, nt=n_tiles)))
        start += n_tiles
    if emit:
        shp, spec = _wb_out(k, start * tn, tn)
        out_shape.append(shp)
        out_specs.append(spec)
    return pl.pallas_call(
        functools.partial(_norm_proj_kernel, n_extra=len(extras), segs=tuple(ksegs), emit=emit),
        grid=(m // tm, start), in_specs=in_specs, out_specs=out_specs, out_shape=out_shape,
        scratch_shapes=[pltpu.VMEM((tm, k), BF16)],
        compiler_params=_params("arbitrary", "arbitrary"), name=name)(*args)


def _ep_swiglu(accs, extra, outs):
    a, b = accs
    outs[0][...] = (a * jax.nn.sigmoid(a) * b).astype(BF16)


def _ep_sb_q(accs, extra, outs, *, n_heads, scale):
    g = extra[0][...]
    for h in range(n_heads):
        sl = slice(h * SB_HEAD_DIM, (h + 1) * SB_HEAD_DIM)
        outs[0][:, sl] = (_rms(accs[0][:, sl], g) * scale).astype(BF16)


def _ep_sb_k(accs, extra, outs, *, n_heads):
    g = extra[1][...]
    for h in range(n_heads):
        sl = slice(h * SB_HEAD_DIM, (h + 1) * SB_HEAD_DIM)
        kn = _rms(accs[0][:, sl], g)
        outs[0][:, sl] = kn
        outs[1][:, sl] = kn.astype(BF16)


def _ep_copy2(accs, extra, outs):
    outs[0][...] = accs[0]
    outs[1][...] = accs[0].astype(BF16)


def _ep_bf16(accs, extra, outs):
    outs[0][...] = accs[0].astype(BF16)


def _ep_silu(accs, extra, outs):
    a = accs[0]
    outs[0][...] = (a * jax.nn.sigmoid(a)).astype(BF16)


def _ep_rotary(accs, extra, outs, *, n_heads, scale):
    cos = extra[0][...]
    sin = extra[1][...]
    half = RET_DK // 2
    for h in range(n_heads):
        x1 = accs[0][:, h * RET_DK:h * RET_DK + half]
        x2 = accs[0][:, h * RET_DK + half:(h + 1) * RET_DK]
        outs[0][:, h * RET_DK:h * RET_DK + half] = ((x1 * cos - x2 * sin) * scale).astype(BF16)
        outs[0][:, h * RET_DK + half:(h + 1) * RET_DK] = ((x1 * sin + x2 * cos) * scale).astype(BF16)


def _mm_res_kernel(a_ref, w_ref, r_ref, o_ref, *wb_ref, scale):
    wb = w_ref[...].astype(BF16)
    if wb_ref:
        wb_ref[0][...] = wb
    o_ref[...] = r_ref[...] + scale * _dot(a_ref[...], wb)


def _mm_res(a, w, lead, res, scale, emit=False, tm=1024, tn=512, name="mm_res"):
    m, k = a.shape
    n = w.shape[-1]
    tm = min(tm, m)
    assert m % tm == 0 and n % tn == 0 and (not emit or m == tm)
    out_shape = [jax.ShapeDtypeStruct((m, n), F32)]
    out_specs = [pl.BlockSpec((tm, tn), lambda i, j: (i, j))]
    if emit:
        shp, spec = _wb_out(k, n, tn)
        out_shape.append(shp)
        out_specs.append(spec)
    out = pl.pallas_call(
        functools.partial(_mm_res_kernel, scale=scale),
        grid=(m // tm, n // tn),
        in_specs=[pl.BlockSpec((tm, k), lambda i, j: (i, 0)),
                  _w_spec(w, lead, k, tn, lambda i, j: j),
                  pl.BlockSpec((tm, tn), lambda i, j: (i, j))],
        out_specs=out_specs, out_shape=out_shape,
        compiler_params=_params("arbitrary", "arbitrary"), name=name)(a, w, res)
    return out if emit else out[0]


def _ffn(x, gain, w13, w2, emit=False):
    w2a, lead2 = w2
    d_ff = w2a.shape[-2]
    if isinstance(w13[0], tuple):
        views = [(w13[0][0], (), 0), (w13[0][1], (), 0)]
    else:
        views = [(w13[0], w13[1], 0), (w13[0], w13[1], d_ff)]
    up = _norm_mm(x, gain, views, d_ff, _ep_swiglu, (BF16,), emit=emit, name="ffn_up")
    down = _mm_res(up[0], w2a, lead2, x, 0.5, emit=emit, name="ffn_down")
    if emit:
        return down[0], ((up[1], up[2]), ()), (down[1], ())
    return down


def _pool_kernel(x_ref, g_ref, buf_ref, w_ref, sc_ref, o_ref, xn_ref, xe_sc, *, ts, pos0):
    s = pl.program_id(1)

    @pl.when(s == 0)
    def _():
        xe_sc[0:POOL_HALO, :] = buf_ref[0]

    x = x_ref[0]
    xn = _rms(x, g_ref[...])
    xn_ref[0] = xn
    xe_sc[POOL_HALO:POOL_HALO + ts, :] = xn

    gd = x.shape[1] // len(POOL_WINDOWS)
    pos = pos0 + s * ts + lax.broadcasted_iota(jnp.int32, (ts, 1), 0)
    for gi, win in enumerate(POOL_WINDOWS):
        sl = slice(gi * gd, (gi + 1) * gd)
        acc = xe_sc[:, sl]
        step = 1
        while step < win:
            acc = acc + pltpu.roll(acc, step, 0)
            step *= 2
        cnt = jnp.minimum(win, pos + 1).astype(F32)
        pooled = acc[POOL_HALO:, :] / cnt
        d = (pooled - xn[:, sl]).astype(BF16)
        y = _dot(d, w_ref[gi])
        o_ref[0, :, sl] = x[:, sl] + y * sc_ref[:, sl]

    xe_sc[0:POOL_HALO, :] = xe_sc[ts:ts + POOL_HALO, :]


def _pool_mixer(x, gain, buf, w, scale, pos0, ts):
    b, s, d = x.shape
    ts = min(ts, s)
    assert s % ts == 0
    gd = d // len(POOL_WINDOWS)
    return pl.pallas_call(
        functools.partial(_pool_kernel, ts=ts, pos0=pos0),
        grid=(b, s // ts),
        in_specs=[pl.BlockSpec((1, ts, d), lambda i, j: (i, j, 0)),
                  pl.BlockSpec((1, d), lambda i, j: (0, 0)),
                  pl.BlockSpec((1, POOL_HALO, d), lambda i, j: (i, 0, 0)),
                  pl.BlockSpec((len(POOL_WINDOWS), gd, gd), lambda i, j: (0, 0, 0)),
                  pl.BlockSpec((1, d), lambda i, j: (0, 0))],
        out_specs=[pl.BlockSpec((1, ts, d), lambda i, j: (i, j, 0)),
                   pl.BlockSpec((1, ts, d), lambda i, j: (i, j, 0))],
        out_shape=[jax.ShapeDtypeStruct((b, s, d), F32), jax.ShapeDtypeStruct((b, s, d), F32)],
        scratch_shapes=[pltpu.VMEM((ts + POOL_HALO, d), F32)],
        compiler_params=_params("arbitrary", "arbitrary"), name="pool_mixer",
    )(x, gain.reshape(1, d), buf, w, scale.reshape(1, d))


def _sb_softplus2(z, mask):
    sp = jnp.maximum(z, 0.0) + jnp.log2(1.0 + jnp.exp2(-jnp.abs(z)))
    return sp if mask is None else jnp.where(mask, sp, 0.0)


def _sb_neg_cumsum(sp, tri_neg):
    hi = sp.astype(BF16)
    lo = (sp - hi.astype(F32)).astype(BF16)
    return _dot(jnp.concatenate([hi, lo], axis=1), tri_neg)


def _sb_weights(z, tri_neg, carry, mask):
    sp = _sb_softplus2(z, mask)
    a = jnp.exp2(z + _sb_neg_cumsum(sp, tri_neg) + carry)
    if mask is not None:
        a = jnp.where(mask, a, 0.0)
    return a, carry - jnp.sum(sp, axis=-1, keepdims=True)


def _sb_prompt_kernel(bias_ref, q_ref, k_ref, v_ref, tri_ref, o_ref, *, tq, heads):
    g = pl.program_id(1)
    i = pl.program_id(2)
    tri = tri_ref[...]
    rows = lax.broadcasted_iota(jnp.int32, (tq, tq), 0)
    cols = lax.broadcasted_iota(jnp.int32, (tq, tq), 1)
    hd = SB_HEAD_DIM

    def block(off, state, mask):
        sls = [slice(hh * hd, (hh + 1) * hd) for hh in range(heads)]
        zs = [_dot_nt(q_ref[0, :, sl], k_ref[0, pl.ds(off, tq), sl]) + bias_ref[g * heads + hh] * LOG2_E
              for hh, sl in enumerate(sls)]
        sps = [_sb_softplus2(z, mask) for z in zs]
        css = [_sb_neg_cumsum(sp, tri) for sp in sps]
        out = []
        for hh, sl in enumerate(sls):
            carry, acc = state[hh]
            a = jnp.exp2(zs[hh] + css[hh] + carry)
            if mask is not None:
                a = jnp.where(mask, a, 0.0)
            out.append((carry - jnp.sum(sps[hh], axis=-1, keepdims=True),
                        acc + _dot(a.astype(BF16), v_ref[0, pl.ds(off, tq), sl])))
        return tuple(out)

    state = tuple((jnp.zeros((tq, 1), F32), jnp.zeros((tq, hd), F32)) for _ in range(heads))
    state = block(pl.multiple_of(i * tq, tq), state, cols < rows)
    state = lax.fori_loop(0, i, lambda n, st: block(pl.multiple_of((i - 1 - n) * tq, tq), st, None), state)
    for hh in range(heads):
        o_ref[0, :, hh * hd:(hh + 1) * hd] = state[hh][1].astype(BF16)


def _tri_neg(n):
    r = lax.broadcasted_iota(jnp.int32, (2 * n, n), 0) % n
    c = lax.broadcasted_iota(jnp.int32, (2 * n, n), 1)
    return jnp.where(r >= c, -1.0, 0.0).astype(BF16)


def _sb_prompt(q, k, v, bias, tq=256, heads=4):
    b, s, hd_all = q.shape
    tq = min(tq, s)
    assert s % tq == 0
    n_heads = hd_all // SB_HEAD_DIM
    w = heads * SB_HEAD_DIM
    return pl.pallas_call(
        functools.partial(_sb_prompt_kernel, tq=tq, heads=heads),
        grid=(b, n_heads // heads, s // tq),
        in_specs=[pl.BlockSpec(memory_space=pltpu.SMEM),
                  pl.BlockSpec((1, tq, w), lambda bi, h, i: (bi, i, h)),
                  pl.BlockSpec((1, s, w), lambda bi, h, i: (bi, 0, h)),
                  pl.BlockSpec((1, s, w), lambda bi, h, i: (bi, 0, h)),
                  pl.BlockSpec((2 * tq, tq), lambda bi, h, i: (0, 0))],
        out_specs=pl.BlockSpec((1, tq, w), lambda bi, h, i: (bi, i, h)),
        out_shape=jax.ShapeDtypeStruct((b, s, hd_all), BF16),
        compiler_params=_params("arbitrary", "arbitrary", "arbitrary"), name="sb_prompt",
    )(bias, q, k, v, _tri_neg(tq))


def _sb_sample_kernel(pt_ref, q_ref, brow_ref, kn_ref, vn_ref, *rest, n_heads, pps):
    kc_refs, vc_refs = rest[:pps], rest[pps:2 * pps]
    tri_ref, o_ref, acc_sc, carry_sc = rest[2 * pps:]
    p = pl.program_id(1)
    rows, page = brow_ref.shape
    rq = rows // n_heads
    hd = SB_HEAD_DIM

    def visit(k_ref, v_ref, mask):
        head = lambda ref, h: ref[0, pl.ds(h, page, stride=n_heads), :].astype(BF16)
        zs = [_dot_nt(q_ref[0, :, h * hd:(h + 1) * hd], head(k_ref, h)) for h in range(n_heads)]
        z = jnp.concatenate(zs, axis=0) + brow_ref[...]
        a, carry = _sb_weights(z, tri_ref[...], carry_sc[...], mask)
        carry_sc[...] = carry
        a = a.astype(BF16)
        for h in range(n_heads):
            acc_sc[:, h * hd:(h + 1) * hd] += _dot(a[h * rq:(h + 1) * rq, :], head(v_ref, h))

    @pl.when(p == 0)
    def _():
        acc_sc[...] = jnp.zeros_like(acc_sc)
        carry_sc[...] = jnp.zeros_like(carry_sc)
        t = lax.broadcasted_iota(jnp.int32, (rows, page), 0) % rq
        sk = lax.broadcasted_iota(jnp.int32, (rows, page), 1)
        visit(kn_ref, vn_ref, sk < t)

    @pl.when(p > 0)
    def _():
        for u in range(pps):
            visit(kc_refs[u], vc_refs[u], None)

    @pl.when(p == pl.num_programs(1) - 1)
    def _():
        o_ref[0] = acc_sc[...].astype(BF16)


def _sb_sample(q, k_new, v_new, cache_k, cache_v, page_table, bias, pps=8):
    b, r, hd_all = q.shape
    n_pages = page_table.shape[1]
    _, page, n_heads, hd = cache_k.shape
    assert n_pages % pps == 0
    rows = n_heads * r
    brow = jnp.broadcast_to(jnp.repeat(bias.astype(F32) * LOG2_E, r)[:, None], (rows, page))
    pad = ((0, 0), (0, page - r), (0, 0), (0, 0))
    flat = lambda c: c.reshape(c.shape[0], page * n_heads, hd)

    def cache_map(u, bi, p, pt):
        return (pt[bi, n_pages - 1 - ((jnp.maximum(p, 1) - 1) * pps + u)], 0, 0)

    blk = (1, page * n_heads, hd)
    cache_specs = [pl.BlockSpec(blk, functools.partial(cache_map, u)) for u in range(pps)]
    grid_spec = pltpu.PrefetchScalarGridSpec(
        num_scalar_prefetch=1,
        grid=(b, n_pages // pps + 1),
        in_specs=[pl.BlockSpec((1, r, hd_all), lambda bi, p, pt: (bi, 0, 0)),
                  pl.BlockSpec((rows, page), lambda bi, p, pt: (0, 0)),
                  pl.BlockSpec(blk, lambda bi, p, pt: (bi, 0, 0)),
                  pl.BlockSpec(blk, lambda bi, p, pt: (bi, 0, 0))] + cache_specs + cache_specs + [
                  pl.BlockSpec((2 * page, page), lambda bi, p, pt: (0, 0))],
        out_specs=pl.BlockSpec((1, r, hd_all), lambda bi, p, pt: (bi, 0, 0)),
        scratch_shapes=[pltpu.VMEM((r, hd_all), F32), pltpu.VMEM((rows, 1), F32)])
    ck, cv = flat(cache_k), flat(cache_v)
    return pl.pallas_call(
        functools.partial(_sb_sample_kernel, n_heads=n_heads, pps=pps),
        grid_spec=grid_spec,
        out_shape=jax.ShapeDtypeStruct((b, r, hd_all), BF16),
        compiler_params=_params("arbitrary", "arbitrary"), name="sb_sample",
    )(page_table, q, brow, flat(jnp.pad(k_new, pad)), flat(jnp.pad(v_new, pad)),
      *([ck] * pps), *([cv] * pps), _tri_neg(page))


def _ret_kernel(dc_ref, q_ref, k_ref, v_ref, g_ref, s0_ref, din_ref, dq_ref, dk_ref, y_ref, so_ref, st_sc, *,
                heads):
    hg = pl.program_id(1)
    c = pl.program_id(2)

    @pl.when(c == 0)
    def _():
        st_sc[...] = s0_ref[0]

    for hh in range(heads):
        q = q_ref[0, :, hh * RET_DK:(hh + 1) * RET_DK]
        k = k_ref[0, :, hh * RET_DK:(hh + 1) * RET_DK]
        v = v_ref[0, :, hh * RET_DV:(hh + 1) * RET_DV]
        state = st_sc[hh]
        att = _dot_nt(q, k) * din_ref[hh]
        o = _dot(att.astype(BF16), v)
        cross = _dot(q, state.astype(BF16))
        dq = dq_ref[hh]
        dk = dk_ref[hh]
        lanes = dq.shape[1]
        o = o + cross * jnp.concatenate([dq] * (RET_DV // lanes), axis=1)
        kd = (k.astype(F32) * jnp.concatenate([dk] * (RET_DK // lanes), axis=1)).astype(BF16)
        st_sc[hh] = state * dc_ref[hg * heads + hh] + _dot_tn(kd, v)
        o = o * lax.rsqrt(jnp.mean(o * o, axis=-1, keepdims=True) + NORM_EPS)
        gate = g_ref[0, :, hh * RET_DV:(hh + 1) * RET_DV].astype(F32)
        y_ref[0, :, hh * RET_DV:(hh + 1) * RET_DV] = (gate * o).astype(BF16)

    @pl.when(c == pl.num_programs(2) - 1)
    def _():
        so_ref[0] = st_sc[...]


def _retention(q, k, v, g, s0, chunk, c_true, heads=2):
    b, s, _ = q.shape
    assert s % chunk == 0 and RET_HEADS % heads == 0
    lg = jnp.log1p(-(2.0 ** (-5.0 - jnp.arange(RET_HEADS, dtype=F32))))
    idx = jnp.arange(chunk, dtype=F32)
    diff = idx[:, None] - idx[None, :]
    din = jnp.exp(jnp.where(diff >= 0, diff[None] * lg[:, None, None], -jnp.inf))
    dq = jnp.exp((idx + 1.0)[None, :] * lg[:, None])
    dk = jnp.where(idx[None, :] < c_true, jnp.exp((c_true - 1.0 - idx)[None, :] * lg[:, None]), 0.0)
    dc = jnp.exp(c_true * lg)
    lanes = 128
    dq = jnp.broadcast_to(dq[:, :, None], (RET_HEADS, chunk, lanes))
    dk = jnp.broadcast_to(dk[:, :, None], (RET_HEADS, chunk, lanes))
    wk, wv = heads * RET_DK, heads * RET_DV
    return pl.pallas_call(
        functools.partial(_ret_kernel, heads=heads),
        grid=(b, RET_HEADS // heads, s // chunk),
        in_specs=[pl.BlockSpec(memory_space=pltpu.SMEM),
                  pl.BlockSpec((1, chunk, wk), lambda bi, h, c: (bi, c, h)),
                  pl.BlockSpec((1, chunk, wk), lambda bi, h, c: (bi, c, h)),
                  pl.BlockSpec((1, chunk, wv), lambda bi, h, c: (bi, c, h)),
                  pl.BlockSpec((1, chunk, wv), lambda bi, h, c: (bi, c, h)),
                  pl.BlockSpec((1, heads, RET_DK, RET_DV), lambda bi, h, c: (bi, h, 0, 0)),
                  pl.BlockSpec((heads, chunk, chunk), lambda bi, h, c: (h, 0, 0)),
                  pl.BlockSpec((heads, chunk, lanes), lambda bi, h, c: (h, 0, 0)),
                  pl.BlockSpec((heads, chunk, lanes), lambda bi, h, c: (h, 0, 0))],
        out_specs=[pl.BlockSpec((1, chunk, wv), lambda bi, h, c: (bi, c, h)),
                   pl.BlockSpec((1, heads, RET_DK, RET_DV), lambda bi, h, c: (bi, h, 0, 0))],
        out_shape=[jax.ShapeDtypeStruct((b, s, RET_HEADS * RET_DV), BF16),
                   jax.ShapeDtypeStruct(s0.shape, F32)],
        scratch_shapes=[pltpu.VMEM((heads, RET_DK, RET_DV), F32)],
        compiler_params=_params("arbitrary", "arbitrary", "arbitrary"), name="retention",
    )(dc, q, k, v, g, s0, din, dq, dk)


def _conv_kernel(x_ref, g_ref, wb_ref, wc_ref, wh_ref, cw_ref, buf_ref, y_ref, u_ref, *rest, ts):
    ue_sc = rest[-1]
    s = pl.program_id(2)
    for o_ref, w_ref in zip(rest[:-1], (wb_ref, wc_ref, wh_ref)):
        o_ref[...] = w_ref[...].astype(BF16)

    @pl.when(s == 0)
    def _():
        ue_sc[0:CONV_HALO, :] = buf_ref[0]

    xn = _rms(x_ref[0], g_ref[...]).astype(BF16)
    bb = _dot(xn, wb_ref[...].astype(BF16))
    u = _dot(xn, wc_ref[...].astype(BF16)) * _dot(xn, wh_ref[...].astype(BF16))
    u_ref[0] = u
    ue_sc[CONV_HALO:CONV_HALO + ts, :] = u
    ue = ue_sc[...]
    conv = cw_ref[CONV_WIDTH - 1:CONV_WIDTH, :] * u
    for t in range(CONV_WIDTH - 1):
        shift = CONV_WIDTH - 1 - t
        conv = conv + cw_ref[t:t + 1, :] * pltpu.roll(ue, shift, 0)[CONV_HALO:, :]
    y_ref[0] = (bb * conv).astype(BF16)
    ue_sc[0:CONV_HALO, :] = ue_sc[ts:ts + CONV_HALO, :]


def _short_conv(x, gain, views, conv_w, buf, ts, emit=False, tn=512):
    b, s, d = x.shape
    ts = min(ts, s)
    assert s % ts == 0 and d % tn == 0 and all(off % tn == 0 for _, _, off in views)
    nb = d // tn
    w_specs = [pl.BlockSpec((None,) * len(lead) + (d, tn),
                            functools.partial(lambda n, bi, si, lead, o: tuple(lead) + (0, n + o), lead=lead, o=off // tn))
               for _, lead, off in views]
    out_specs = [pl.BlockSpec((1, ts, tn), lambda n, bi, si: (bi, si, n)),
                 pl.BlockSpec((1, ts, tn), lambda n, bi, si: (bi, si, n))]
    out_shape = [jax.ShapeDtypeStruct((b, s, d), BF16), jax.ShapeDtypeStruct((b, s, d), F32)]
    if emit:
        out_specs += [pl.BlockSpec((d, tn), lambda n, bi, si: (0, n))] * 3
        out_shape += [jax.ShapeDtypeStruct((d, d), BF16)] * 3
    return pl.pallas_call(
        functools.partial(_conv_kernel, ts=ts),
        grid=(nb, b, s // ts),
        in_specs=[pl.BlockSpec((1, ts, d), lambda n, bi, si: (bi, si, 0)),
                  pl.BlockSpec((1, d), lambda n, bi, si: (0, 0))] + w_specs + [
                  pl.BlockSpec((CONV_WIDTH, tn), lambda n, bi, si: (0, n)),
                  pl.BlockSpec((1, CONV_HALO, tn), lambda n, bi, si: (bi, 0, n))],
        out_specs=out_specs, out_shape=out_shape,
        scratch_shapes=[pltpu.VMEM((ts + CONV_HALO, tn), F32)],
        compiler_params=_params("arbitrary", "arbitrary", "arbitrary"), name="short_conv",
    )(x, gain.reshape(1, d), *[w for w, _, _ in views], conv_w, buf)


def _rope_tables(pos):
    inv = ROPE_BASE ** (-jnp.arange(0, RET_DK, 2, dtype=F32) / RET_DK)
    ang = pos.astype(F32)[:, None] * inv[None, :]
    return jnp.cos(ang), jnp.sin(ang)


def _trunk(x, pos0, n_valid, pool_buf, ret_s0, conv_buf, paged, p, wts, ts):
    emit = wts is None
    made = {}
    b, s, d = x.shape
    m = b * s
    depth = p["ffn_norm"].shape[0]
    pools, ks, vs, rets, convs = [], [], [], [], []
    x = x.reshape(m, d)

    def ffn(x, i, k):
        if emit:
            x, made["ffn_w13", i, k], made["ffn_w2", i, k] = _ffn(
                x, p["ffn_norm"][i, k], (p["ffn_w13"], (i, k)), (p["ffn_w2"], (i, k)), emit=True)
            return x
        return _ffn(x, p["ffn_norm"][i, k], wts["ffn_w13", i, k], wts["ffn_w2", i, k])

    def out_proj(y, name, j, x):
        if emit:
            x, wb = _mm_res(y, p[name], (j,), x, 1.0, emit=True, name=name)
            made[name, j] = (wb, ())
            return x
        return _mm_res(y, *wts[name, j], x, 1.0, name=name)

    for i in range(depth):
        j, mixer = i // 4, i % 4
        x = ffn(x, i, 0)
        gain = p["mix_norm"][i]
        if mixer == 0:
            buf = jnp.pad(pool_buf[j], ((0, 0), (POOL_HALO - pool_buf.shape[2], 0), (0, 0)))
            x3, xn = _pool_mixer(x.reshape(b, s, d), gain, buf, p["pool_w"][j].astype(BF16), p["pool_scale"][j],
                                 pos0, ts)
            x = x3.reshape(m, d)
            pools.append(jnp.concatenate([pool_buf[j], xn[:, :n_valid]], axis=1)[:, -pool_buf.shape[2]:])
        elif mixer == 1:
            hd_all = SB_HEADS * SB_HEAD_DIM
            gspec = pl.BlockSpec((1, SB_HEAD_DIM), lambda i_, j_: (0, 0))
            n_h = 512 // SB_HEAD_DIM
            w, lead = (p["sb_w_qkv"], (j,)) if emit else wts["sb_w_qkv", j]
            res = _norm_proj(
                x, gain, w, lead,
                [(hd_all, functools.partial(_ep_sb_q, n_heads=n_h, scale=LOG2_E / math.sqrt(SB_HEAD_DIM)), (BF16,)),
                 (hd_all, functools.partial(_ep_sb_k, n_heads=n_h), (F32, BF16)),
                 (hd_all, _ep_copy2, (F32, BF16))],
                extras=[(p["sb_q_norm"][j].reshape(1, -1), gspec), (p["sb_k_norm"][j].reshape(1, -1), gspec)],
                emit=emit, name="sb_qkv")
            q, k32, k16, v32, v16 = res[:5]
            if emit:
                made["sb_w_qkv", j] = (res[5], ())
            shp = (b, s, SB_HEADS, SB_HEAD_DIM)
            ks.append(k32.reshape(shp)[:, :n_valid])
            vs.append(v32.reshape(shp)[:, :n_valid])
            if paged:
                o = _sb_sample(q.reshape(b, s, hd_all), k32.reshape(shp), v32.reshape(shp),
                               p["cache_sb_k"][j], p["cache_sb_v"][j], p["page_table"], p["sb_bias"][j])
            else:
                o = _sb_prompt(q.reshape(b, s, hd_all), k16.reshape(b, s, hd_all), v16.reshape(b, s, hd_all),
                               p["sb_bias"][j].astype(F32))
            x = out_proj(o.reshape(m, hd_all), "sb_w_o", j, x)
        elif mixer == 2:
            hk, hv = RET_HEADS * RET_DK, RET_HEADS * RET_DV
            pos = pos0 + jnp.arange(s, dtype=jnp.int32)
            cos, sin = _rope_tables(jnp.tile(pos, b))
            tspec = pl.BlockSpec((min(1024, m), RET_DK // 2), lambda i_, j_: (i_, 0))
            tn = 1024
            rot = lambda sc: functools.partial(_ep_rotary, n_heads=tn // RET_DK, scale=sc)
            w, lead = (p["ret_w_qkvg"], (j,)) if emit else wts["ret_w_qkvg", j]
            res = _norm_proj(
                x, gain, w, lead,
                [(hk, rot(1.0), (BF16,)), (hk, rot(RET_DK ** -0.5), (BF16,)), (hv, _ep_bf16, (BF16,)),
                 (hv, _ep_silu, (BF16,))],
                extras=[(cos, tspec), (sin, tspec)], emit=emit, tn=tn, name="ret_qkvg")
            q, k, v, g = res[:4]
            if emit:
                made["ret_w_qkvg", j] = (res[4], ())
            chunk = min(256, s)
            c_true = min(chunk, n_valid)
            y, s_new = _retention(q.reshape(b, s, hk), k.reshape(b, s, hk), v.reshape(b, s, hv),
                                  g.reshape(b, s, hv), ret_s0[j].astype(F32), chunk, c_true)
            rets.append(s_new)
            x = out_proj(y.reshape(m, hv), "ret_w_o", j, x)
        else:
            cb = conv_buf[j]
            buf = jnp.pad(cb, ((0, 0), (CONV_HALO - cb.shape[1], 0), (0, 0)))
            views = [(p["conv_w_in"], (j,), t * d) for t in range(3)] if emit else wts["conv_w_in", j]
            res = _short_conv(x.reshape(b, s, d), gain, views, p["conv_w"][j], buf, ts, emit=emit)
            if emit:
                made["conv_w_in", j] = [(wb, (), 0) for wb in res[2:]]
            convs.append(jnp.concatenate([cb, res[1][:, :n_valid]], axis=1)[:, -cb.shape[1]:])
            x = out_proj(res[0].reshape(m, d), "conv_w_out", j, x)
        x = ffn(x, i, 1)
    x = x.reshape(b, s, d)[:, :n_valid]
    outs = (x, jnp.stack(pools), jnp.stack(ks), jnp.stack(vs), jnp.stack(rets), jnp.stack(convs))
    return outs, made


def kernel(x_prompt, x_sample, state_pool, cache_sb_k, cache_sb_v, page_table, state_ret, state_conv,
           ffn_norm, ffn_w13, ffn_w2, mix_norm, pool_w, pool_scale, sb_w_qkv, sb_q_norm, sb_k_norm,
           sb_bias, sb_w_o, ret_w_qkvg, ret_w_o, conv_w_in, conv_w, conv_w_out):
    p = dict(
        ffn_norm=ffn_norm, ffn_w13=ffn_w13, ffn_w2=ffn_w2, mix_norm=mix_norm, pool_w=pool_w,
        pool_scale=pool_scale, sb_w_qkv=sb_w_qkv, sb_q_norm=sb_q_norm, sb_k_norm=sb_k_norm, sb_bias=sb_bias,
        sb_w_o=sb_w_o, ret_w_qkvg=ret_w_qkvg, ret_w_o=ret_w_o, conv_w_in=conv_w_in, conv_w=conv_w,
        conv_w_out=conv_w_out, cache_sb_k=cache_sb_k, cache_sb_v=cache_sb_v, page_table=page_table)

    bs, ss, _ = x_sample.shape
    past_len = page_table.shape[1] * cache_sb_k.shape[2]
    xs = jnp.pad(x_sample, ((0, 0), (0, SAMPLE_ROWS - ss), (0, 0)))
    (y_sample, pool_s, k_s, v_s, ret_s, conv_s), wts = _trunk(
        xs, past_len, ss, state_pool, state_ret, state_conv, True, p, None, SAMPLE_ROWS)

    bp, sp, d = x_prompt.shape
    (y_prompt, pool_p, k_p, v_p, ret_p, conv_p), _ = _trunk(
        x_prompt, 0, sp,
        jnp.zeros((state_pool.shape[0], bp) + state_pool.shape[2:], F32),
        jnp.zeros((state_ret.shape[0], bp) + state_ret.shape[2:], F32),
        jnp.zeros((state_conv.shape[0], bp) + state_conv.shape[2:], F32),
        False, p, wts, 256)
    return (y_prompt, y_sample, pool_p, pool_s, k_p, v_p, k_s, v_s, ret_p, ret_s, conv_p, conv_s)
```

```python
import functools
import math

import jax
import jax.numpy as jnp
from jax import lax
from jax.experimental import pallas as pl
from jax.experimental.pallas import tpu as pltpu

F32 = jnp.float32
BF16 = jnp.bfloat16

NORM_EPS = 1e-6
POOL_WINDOWS = (2, 4, 8, 16)
POOL_HALO = 16
SB_HEADS = 16
SB_HEAD_DIM = 128
RET_HEADS = 8
RET_DK = 256
RET_DV = 512
ROPE_BASE = 10000.0
LOG2_E = 1.4426950408889634
CONV_WIDTH = 3
CONV_HALO = 8
SAMPLE_ROWS = 16

VMEM_LIMIT_BYTES = 56 * 1024 * 1024


def _params(*sem):
    return pltpu.CompilerParams(dimension_semantics=sem, vmem_limit_bytes=VMEM_LIMIT_BYTES)


def _rms(x, g):
    ms = jnp.mean(x * x, axis=-1, keepdims=True)
    return x * lax.rsqrt(ms + NORM_EPS) * g


def _dot(a, b):
    return jnp.dot(a, b, preferred_element_type=F32)


def _dot_nt(a, b):
    return lax.dot_general(a, b, (((1,), (1,)), ((), ())), preferred_element_type=F32)


def _dot_tn(a, b):
    return lax.dot_general(a, b, (((0,), (0,)), ((), ())), preferred_element_type=F32)


NORM_CHUNKS = 4


def _norm_mm_kernel(*refs, n_w, n_extra, n_out, epilogue, emit):
    x_ref, g_ref = refs[0], refs[1]
    w_refs = refs[2:2 + n_w]
    extra = refs[2 + n_w:2 + n_w + n_extra]
    outs = refs[2 + n_w + n_extra:2 + n_w + n_extra + n_out]
    wb_refs = refs[2 + n_w + n_extra + n_out:-1]
    xn_sc = refs[-1]
    j = pl.program_id(1)
    if emit:
        for wb_ref, w in zip(wb_refs, w_refs):
            wb_ref[...] = w[...].astype(BF16)

    tm = x_ref.shape[0]
    chunks = NORM_CHUNKS if tm % (NORM_CHUNKS * 256) == 0 else 1
    rows = tm // chunks

    @pl.when(j == 0)
    def _():
        ws = [w[...].astype(BF16) for w in w_refs]
        for r in range(chunks):
            sl = pl.ds(r * rows, rows)
            xn = _rms(x_ref[sl, :], g_ref[...]).astype(BF16)
            xn_sc[sl, :] = xn
            epilogue([_dot(xn, wb) for wb in ws], extra, [o.at[sl, :] for o in outs])

    @pl.when(j > 0)
    def _():
        xn = xn_sc[...]
        epilogue([_dot(xn, w[...].astype(BF16)) for w in w_refs], extra, outs)


def _w_spec(w, lead, k, tn, col_block):
    assert w.ndim == len(lead) + 2 and w.shape[-2] == k
    return pl.BlockSpec((None,) * len(lead) + (k, tn), lambda i, j: tuple(lead) + (0, col_block(i, j)))


def _wb_out(k, n_cols, tn):
    return jax.ShapeDtypeStruct((k, n_cols), BF16), pl.BlockSpec((k, tn), lambda i, j: (0, j))


def _norm_mm(x, gain, views, n_cols, epilogue, out_dtypes, extras=(), emit=False, tm=1024, tn=512,
             name="norm_mm"):
    m, k = x.shape
    tm = min(tm, m)
    assert m % tm == 0 and n_cols % tn == 0 and all(off % tn == 0 for _, _, off in views)
    assert not emit or m == tm
    grid = (m // tm, n_cols // tn)
    in_specs = [pl.BlockSpec((tm, k), lambda i, j: (i, 0)),
                pl.BlockSpec((1, k), lambda i, j: (0, 0))]
    args = [x, gain.reshape(1, k)]
    for w, lead, off in views:
        in_specs.append(_w_spec(w, lead, k, tn, functools.partial(lambda i, j, o: j + o, o=off // tn)))
        args.append(w)
    for arr, spec in extras:
        in_specs.append(spec)
        args.append(arr)
    out_shape = [jax.ShapeDtypeStruct((m, n_cols), dt) for dt in out_dtypes]
    out_specs = [pl.BlockSpec((tm, tn), lambda i, j: (i, j)) for _ in out_dtypes]
    if emit:
        for _ in views:
            shp, spec = _wb_out(k, n_cols, tn)
            out_shape.append(shp)
            out_specs.append(spec)
    kern = functools.partial(_norm_mm_kernel, n_w=len(views), n_extra=len(extras), n_out=len(out_dtypes),
                             epilogue=epilogue, emit=emit)
    return pl.pallas_call(
        kern, grid=grid, in_specs=in_specs, out_specs=out_specs, out_shape=out_shape,
        scratch_shapes=[pltpu.VMEM((tm, k), BF16)],
        compiler_params=_params("arbitrary", "arbitrary"), name=name)(*args)


def _norm_proj_kernel(*refs, n_extra, segs, emit):
    x_ref, g_ref, w_ref = refs[:3]
    extra = refs[3:3 + n_extra]
    outs = refs[3 + n_extra:-1]
    xn_sc = refs[-1]
    j = pl.program_id(1)

    @pl.when(j == 0)
    def _():
        xn_sc[...] = _rms(x_ref[...], g_ref[...]).astype(BF16)

    if emit:
        outs[-1][...] = w_ref[...].astype(BF16)
    first = 0
    for start, n_tiles, epilogue, out_dtypes in segs:
        seg_outs = outs[first:first + len(out_dtypes)]
        first += len(out_dtypes)

        @pl.when((j >= start) & (j < start + n_tiles))
        def _(epilogue=epilogue, seg_outs=seg_outs):
            epilogue([_dot(xn_sc[...], w_ref[...].astype(BF16))], extra, seg_outs)


def _norm_proj(x, gain, w, lead, segs, extras=(), emit=False, tm=1024, tn=512, name="norm_proj"):
    m, k = x.shape
    tm = min(tm, m)
    assert m % tm == 0 and all(n % tn == 0 for n, _, _ in segs) and (not emit or m == tm)
    in_specs = [pl.BlockSpec((tm, k), lambda i, j: (i, 0)),
                pl.BlockSpec((1, k), lambda i, j: (0, 0)),
                _w_spec(w, lead, k, tn, lambda i, j: j)]
    args = [x, gain.reshape(1, k), w]
    for arr, spec in extras:
        in_specs.append(spec)
        args.append(arr)
    ksegs, out_shape, out_specs, start = [], [], [], 0
    for n_cols, epilogue, out_dtypes in segs:
        n_tiles = n_cols // tn
        ksegs.append((start, n_tiles, epilogue, out_dtypes))
        for dt in out_dtypes:
            out_shape.append(jax.ShapeDtypeStruct((m, n_cols), dt))
            out_specs.append(pl.BlockSpec(
                (tm, tn), functools.partial(lambda i, j, s0, nt: (i, jnp.clip(j - s0, 0, nt - 1)), s0=start, nt=n_tiles)))
        start += n_tiles
    if emit:
        shp, spec = _wb_out(k, start * tn, tn)
        out_shape.append(shp)
        out_specs.append(spec)
    return pl.pallas_call(
        functools.partial(_norm_proj_kernel, n_extra=len(extras), segs=tuple(ksegs), emit=emit),
        grid=(m // tm, start), in_specs=in_specs, out_specs=out_specs, out_shape=out_shape,
        scratch_shapes=[pltpu.VMEM((tm, k), BF16)],
        compiler_params=_params("arbitrary", "arbitrary"), name=name)(*args)


def _ep_swiglu(accs, extra, outs):
    a, b = accs
    outs[0][...] = (a * jax.nn.sigmoid(a) * b).astype(BF16)


def _ep_sb_q(accs, extra, outs, *, n_heads, scale):
    g = extra[0][...]
    for h in range(n_heads):
        sl = slice(h * SB_HEAD_DIM, (h + 1) * SB_HEAD_DIM)
        outs[0][:, sl] = (_rms(accs[0][:, sl], g) * scale).astype(BF16)


def _ep_sb_k(accs, extra, outs, *, n_heads):
    g = extra[1][...]
    for h in range(n_heads):
        sl = slice(h * SB_HEAD_DIM, (h + 1) * SB_HEAD_DIM)
        kn = _rms(accs[0][:, sl], g)
        outs[0][:, sl] = kn
        outs[1][:, sl] = kn.astype(BF16)


def _ep_copy2(accs, extra, outs):
    outs[0][...] = accs[0]
    outs[1][...] = accs[0].astype(BF16)


def _ep_bf16(accs, extra, outs):
    outs[0][...] = accs[0].astype(BF16)


def _ep_silu(accs, extra, outs):
    a = accs[0]
    outs[0][...] = (a * jax.nn.sigmoid(a)).astype(BF16)


def _ep_rotary(accs, extra, outs, *, n_heads, scale):
    cos = extra[0][...]
    sin = extra[1][...]
    half = RET_DK // 2
    for h in range(n_heads):
        x1 = accs[0][:, h * RET_DK:h * RET_DK + half]
        x2 = accs[0][:, h * RET_DK + half:(h + 1) * RET_DK]
        outs[0][:, h * RET_DK:h * RET_DK + half] = ((x1 * cos - x2 * sin) * scale).astype(BF16)
        outs[0][:, h * RET_DK + half:(h + 1) * RET_DK] = ((x1 * sin + x2 * cos) * scale).astype(BF16)


def _mm_res_kernel(a_ref, w_ref, r_ref, o_ref, *wb_ref, scale):
    wb = w_ref[...].astype(BF16)
    if wb_ref:
        wb_ref[0][...] = wb
    o_ref[...] = r_ref[...] + scale * _dot(a_ref[...], wb)


def _mm_res(a, w, lead, res, scale, emit=False, tm=1024, tn=512, name="mm_res"):
    m, k = a.shape
    n = w.shape[-1]
    tm = min(tm, m)
    assert m % tm == 0 and n % tn == 0 and (not emit or m == tm)
    out_shape = [jax.ShapeDtypeStruct((m, n), F32)]
    out_specs = [pl.BlockSpec((tm, tn), lambda i, j: (i, j))]
    if emit:
        shp, spec = _wb_out(k, n, tn)
        out_shape.append(shp)
        out_specs.append(spec)
    out = pl.pallas_call(
        functools.partial(_mm_res_kernel, scale=scale),
        grid=(m // tm, n // tn),
        in_specs=[pl.BlockSpec((tm, k), lambda i, j: (i, 0)),
                  _w_spec(w, lead, k, tn, lambda i, j: j),
                  pl.BlockSpec((tm, tn), lambda i, j: (i, j))],
        out_specs=out_specs, out_shape=out_shape,
        compiler_params=_params("arbitrary", "arbitrary"), name=name)(a, w, res)
    return out if emit else out[0]


def _ffn(x, gain, w13, w2, emit=False):
    w2a, lead2 = w2
    d_ff = w2a.shape[-2]
    if isinstance(w13[0], tuple):
        views = [(w13[0][0], (), 0), (w13[0][1], (), 0)]
    else:
        views = [(w13[0], w13[1], 0), (w13[0], w13[1], d_ff)]
    up = _norm_mm(x, gain, views, d_ff, _ep_swiglu, (BF16,), emit=emit, name="ffn_up")
    down = _mm_res(up[0], w2a, lead2, x, 0.5, emit=emit, tn=256 if emit else 512, name="ffn_down")
    if emit:
        return down[0], ((up[1], up[2]), ()), (down[1], ())
    return down


def _pool_kernel(x_ref, g_ref, buf_ref, w_ref, sc_ref, o_ref, xn_ref, xe_sc, *, ts, pos0):
    s = pl.program_id(1)

    @pl.when(s == 0)
    def _():
        xe_sc[0:POOL_HALO, :] = buf_ref[0]

    x = x_ref[0]
    xn = _rms(x, g_ref[...])
    xn_ref[0] = xn
    xe_sc[POOL_HALO:POOL_HALO + ts, :] = xn

    gd = x.shape[1] // len(POOL_WINDOWS)
    pos = pos0 + s * ts + lax.broadcasted_iota(jnp.int32, (ts, 1), 0)
    for gi, win in enumerate(POOL_WINDOWS):
        sl = slice(gi * gd, (gi + 1) * gd)
        acc = xe_sc[:, sl]
        step = 1
        while step < win:
            acc = acc + pltpu.roll(acc, step, 0)
            step *= 2
        cnt = jnp.minimum(win, pos + 1).astype(F32)
        pooled = acc[POOL_HALO:, :] / cnt
        d = (pooled - xn[:, sl]).astype(BF16)
        y = _dot(d, w_ref[gi])
        o_ref[0, :, sl] = x[:, sl] + y * sc_ref[:, sl]

    xe_sc[0:POOL_HALO, :] = xe_sc[ts:ts + POOL_HALO, :]


def _pool_mixer(x, gain, buf, w, scale, pos0, ts):
    b, s, d = x.shape
    ts = min(ts, s)
    assert s % ts == 0
    gd = d // len(POOL_WINDOWS)
    return pl.pallas_call(
        functools.partial(_pool_kernel, ts=ts, pos0=pos0),
        grid=(b, s // ts),
        in_specs=[pl.BlockSpec((1, ts, d), lambda i, j: (i, j, 0)),
                  pl.BlockSpec((1, d), lambda i, j: (0, 0)),
                  pl.BlockSpec((1, POOL_HALO, d), lambda i, j: (i, 0, 0)),
                  pl.BlockSpec((len(POOL_WINDOWS), gd, gd), lambda i, j: (0, 0, 0)),
                  pl.BlockSpec((1, d), lambda i, j: (0, 0))],
        out_specs=[pl.BlockSpec((1, ts, d), lambda i, j: (i, j, 0)),
                   pl.BlockSpec((1, ts, d), lambda i, j: (i, j, 0))],
        out_shape=[jax.ShapeDtypeStruct((b, s, d), F32), jax.ShapeDtypeStruct((b, s, d), F32)],
        scratch_shapes=[pltpu.VMEM((ts + POOL_HALO, d), F32)],
        compiler_params=_params("arbitrary", "arbitrary"), name="pool_mixer",
    )(x, gain.reshape(1, d), buf, w, scale.reshape(1, d))


def _sb_softplus2(z, mask):
    sp = jnp.maximum(z, 0.0) + jnp.log2(1.0 + jnp.exp2(-jnp.abs(z)))
    return sp if mask is None else jnp.where(mask, sp, 0.0)


def _sb_neg_cumsum(sp, tri_neg):
    hi = sp.astype(BF16)
    lo = (sp - hi.astype(F32)).astype(BF16)
    return _dot(jnp.concatenate([hi, lo], axis=1), tri_neg)


def _sb_weights(z, tri_neg, carry, mask):
    sp = _sb_softplus2(z, mask)
    a = jnp.exp2(z + _sb_neg_cumsum(sp, tri_neg) + carry)
    if mask is not None:
        a = jnp.where(mask, a, 0.0)
    return a, carry - jnp.sum(sp, axis=-1, keepdims=True)


def _sb_prompt_kernel(bias_ref, q_ref, k_ref, v_ref, tri_ref, o_ref, *, tq, heads):
    g = pl.program_id(1)
    i = pl.program_id(2)
    tri = tri_ref[...]
    rows = lax.broadcasted_iota(jnp.int32, (tq, tq), 0)
    cols = lax.broadcasted_iota(jnp.int32, (tq, tq), 1)
    hd = SB_HEAD_DIM

    def block(off, state, mask):
        sls = [slice(hh * hd, (hh + 1) * hd) for hh in range(heads)]
        zs = [_dot_nt(q_ref[0, :, sl], k_ref[0, pl.ds(off, tq), sl]) + bias_ref[g * heads + hh] * LOG2_E
              for hh, sl in enumerate(sls)]
        sps = [_sb_softplus2(z, mask) for z in zs]
        css = [_sb_neg_cumsum(sp, tri) for sp in sps]
        out = []
        for hh, sl in enumerate(sls):
            carry, acc = state[hh]
            a = jnp.exp2(zs[hh] + css[hh] + carry)
            if mask is not None:
                a = jnp.where(mask, a, 0.0)
            out.append((carry - jnp.sum(sps[hh], axis=-1, keepdims=True),
                        acc + _dot(a.astype(BF16), v_ref[0, pl.ds(off, tq), sl])))
        return tuple(out)

    state = tuple((jnp.zeros((tq, 1), F32), jnp.zeros((tq, hd), F32)) for _ in range(heads))
    state = block(pl.multiple_of(i * tq, tq), state, cols < rows)
    state = lax.fori_loop(0, i, lambda n, st: block(pl.multiple_of((i - 1 - n) * tq, tq), st, None), state)
    for hh in range(heads):
        o_ref[0, :, hh * hd:(hh + 1) * hd] = state[hh][1].astype(BF16)


def _tri_neg(n):
    r = lax.broadcasted_iota(jnp.int32, (2 * n, n), 0) % n
    c = lax.broadcasted_iota(jnp.int32, (2 * n, n), 1)
    return jnp.where(r >= c, -1.0, 0.0).astype(BF16)


def _sb_prompt(q, k, v, bias, tq=256, heads=4):
    b, s, hd_all = q.shape
    tq = min(tq, s)
    assert s % tq == 0
    n_heads = hd_all // SB_HEAD_DIM
    w = heads * SB_HEAD_DIM
    return pl.pallas_call(
        functools.partial(_sb_prompt_kernel, tq=tq, heads=heads),
        grid=(b, n_heads // heads, s // tq),
        in_specs=[pl.BlockSpec(memory_space=pltpu.SMEM),
                  pl.BlockSpec((1, tq, w), lambda bi, h, i: (bi, i, h)),
                  pl.BlockSpec((1, s, w), lambda bi, h, i: (bi, 0, h)),
                  pl.BlockSpec((1, s, w), lambda bi, h, i: (bi, 0, h)),
                  pl.BlockSpec((2 * tq, tq), lambda bi, h, i: (0, 0))],
        out_specs=pl.BlockSpec((1, tq, w), lambda bi, h, i: (bi, i, h)),
        out_shape=jax.ShapeDtypeStruct((b, s, hd_all), BF16),
        compiler_params=_params("arbitrary", "arbitrary", "arbitrary"), name="sb_prompt",
    )(bias, q, k, v, _tri_neg(tq))


def _sb_sample_kernel(pt_ref, q_ref, brow_ref, kn_ref, vn_ref, *rest, n_heads, pps):
    kc_refs, vc_refs = rest[:pps], rest[pps:2 * pps]
    tri_ref, o_ref, acc_sc, carry_sc = rest[2 * pps:]
    p = pl.program_id(1)
    rows, page = brow_ref.shape
    rq = rows // n_heads
    hd = SB_HEAD_DIM

    def visit(k_ref, v_ref, mask):
        head = lambda ref, h: ref[0, pl.ds(h, page, stride=n_heads), :].astype(BF16)
        zs = [_dot_nt(q_ref[0, :, h * hd:(h + 1) * hd], head(k_ref, h)) for h in range(n_heads)]
        z = jnp.concatenate(zs, axis=0) + brow_ref[...]
        a, carry = _sb_weights(z, tri_ref[...], carry_sc[...], mask)
        carry_sc[...] = carry
        a = a.astype(BF16)
        for h in range(n_heads):
            acc_sc[:, h * hd:(h + 1) * hd] += _dot(a[h * rq:(h + 1) * rq, :], head(v_ref, h))

    @pl.when(p == 0)
    def _():
        acc_sc[...] = jnp.zeros_like(acc_sc)
        carry_sc[...] = jnp.zeros_like(carry_sc)
        t = lax.broadcasted_iota(jnp.int32, (rows, page), 0) % rq
        sk = lax.broadcasted_iota(jnp.int32, (rows, page), 1)
        visit(kn_ref, vn_ref, sk < t)

    @pl.when(p > 0)
    def _():
        for u in range(pps):
            visit(kc_refs[u], vc_refs[u], None)

    @pl.when(p == pl.num_programs(1) - 1)
    def _():
        o_ref[0] = acc_sc[...].astype(BF16)


def _sb_sample(q, k_new, v_new, cache_k, cache_v, page_table, bias, pps=8):
    b, r, hd_all = q.shape
    n_pages = page_table.shape[1]
    _, page, n_heads, hd = cache_k.shape
    assert n_pages % pps == 0
    rows = n_heads * r
    brow = jnp.broadcast_to(jnp.repeat(bias.astype(F32) * LOG2_E, r)[:, None], (rows, page))
    pad = ((0, 0), (0, page - r), (0, 0), (0, 0))
    flat = lambda c: c.reshape(c.shape[0], page * n_heads, hd)

    def cache_map(u, bi, p, pt):
        return (pt[bi, n_pages - 1 - ((jnp.maximum(p, 1) - 1) * pps + u)], 0, 0)

    blk = (1, page * n_heads, hd)
    cache_specs = [pl.BlockSpec(blk, functools.partial(cache_map, u)) for u in range(pps)]
    grid_spec = pltpu.PrefetchScalarGridSpec(
        num_scalar_prefetch=1,
        grid=(b, n_pages // pps + 1),
        in_specs=[pl.BlockSpec((1, r, hd_all), lambda bi, p, pt: (bi, 0, 0)),
                  pl.BlockSpec((rows, page), lambda bi, p, pt: (0, 0)),
                  pl.BlockSpec(blk, lambda bi, p, pt: (bi, 0, 0)),
                  pl.BlockSpec(blk, lambda bi, p, pt: (bi, 0, 0))] + cache_specs + cache_specs + [
                  pl.BlockSpec((2 * page, page), lambda bi, p, pt: (0, 0))],
        out_specs=pl.BlockSpec((1, r, hd_all), lambda bi, p, pt: (bi, 0, 0)),
        scratch_shapes=[pltpu.VMEM((r, hd_all), F32), pltpu.VMEM((rows, 1), F32)])
    ck, cv = flat(cache_k), flat(cache_v)
    return pl.pallas_call(
        functools.partial(_sb_sample_kernel, n_heads=n_heads, pps=pps),
        grid_spec=grid_spec,
        out_shape=jax.ShapeDtypeStruct((b, r, hd_all), BF16),
        compiler_params=_params("arbitrary", "arbitrary"), name="sb_sample",
    )(page_table, q, brow, flat(jnp.pad(k_new, pad)), flat(jnp.pad(v_new, pad)),
      *([ck] * pps), *([cv] * pps), _tri_neg(page))


def _ret_kernel(dc_ref, q_ref, k_ref, v_ref, g_ref, s0_ref, din_ref, dq_ref, dk_ref, y_ref, so_ref, st_sc, *,
                heads):
    hg = pl.program_id(1)
    c = pl.program_id(2)

    @pl.when(c == 0)
    def _():
        st_sc[...] = s0_ref[0]

    for hh in range(heads):
        q = q_ref[0, :, hh * RET_DK:(hh + 1) * RET_DK]
        k = k_ref[0, :, hh * RET_DK:(hh + 1) * RET_DK]
        v = v_ref[0, :, hh * RET_DV:(hh + 1) * RET_DV]
        state = st_sc[hh]
        att = _dot_nt(q, k) * din_ref[hh]
        o = _dot(att.astype(BF16), v)
        cross = _dot(q, state.astype(BF16))
        dq = dq_ref[hh]
        dk = dk_ref[hh]
        lanes = dq.shape[1]
        o = o + cross * jnp.concatenate([dq] * (RET_DV // lanes), axis=1)
        kd = (k.astype(F32) * jnp.concatenate([dk] * (RET_DK // lanes), axis=1)).astype(BF16)
        st_sc[hh] = state * dc_ref[hg * heads + hh] + _dot_tn(kd, v)
        o = o * lax.rsqrt(jnp.mean(o * o, axis=-1, keepdims=True) + NORM_EPS)
        gate = g_ref[0, :, hh * RET_DV:(hh + 1) * RET_DV].astype(F32)
        y_ref[0, :, hh * RET_DV:(hh + 1) * RET_DV] = (gate * o).astype(BF16)

    @pl.when(c == pl.num_programs(2) - 1)
    def _():
        so_ref[0] = st_sc[...]


def _retention(q, k, v, g, s0, chunk, c_true, heads=2):
    b, s, _ = q.shape
    assert s % chunk == 0 and RET_HEADS % heads == 0
    lg = jnp.log1p(-(2.0 ** (-5.0 - jnp.arange(RET_HEADS, dtype=F32))))
    idx = jnp.arange(chunk, dtype=F32)
    diff = idx[:, None] - idx[None, :]
    din = jnp.exp(jnp.where(diff >= 0, diff[None] * lg[:, None, None], -jnp.inf))
    dq = jnp.exp((idx + 1.0)[None, :] * lg[:, None])
    dk = jnp.where(idx[None, :] < c_true, jnp.exp((c_true - 1.0 - idx)[None, :] * lg[:, None]), 0.0)
    dc = jnp.exp(c_true * lg)
    lanes = 128
    dq = jnp.broadcast_to(dq[:, :, None], (RET_HEADS, chunk, lanes))
    dk = jnp.broadcast_to(dk[:, :, None], (RET_HEADS, chunk, lanes))
    wk, wv = heads * RET_DK, heads * RET_DV
    return pl.pallas_call(
        functools.partial(_ret_kernel, heads=heads),
        grid=(b, RET_HEADS // heads, s // chunk),
        in_specs=[pl.BlockSpec(memory_space=pltpu.SMEM),
                  pl.BlockSpec((1, chunk, wk), lambda bi, h, c: (bi, c, h)),
                  pl.BlockSpec((1, chunk, wk), lambda bi, h, c: (bi, c, h)),
                  pl.BlockSpec((1, chunk, wv), lambda bi, h, c: (bi, c, h)),
                  pl.BlockSpec((1, chunk, wv), lambda bi, h, c: (bi, c, h)),
                  pl.BlockSpec((1, heads, RET_DK, RET_DV), lambda bi, h, c: (bi, h, 0, 0)),
                  pl.BlockSpec((heads, chunk, chunk), lambda bi, h, c: (h, 0, 0)),
                  pl.BlockSpec((heads, chunk, lanes), lambda bi, h, c: (h, 0, 0)),
                  pl.BlockSpec((heads, chunk, lanes), lambda bi, h, c: (h, 0, 0))],
        out_specs=[pl.BlockSpec((1, chunk, wv), lambda bi, h, c: (bi, c, h)),
                   pl.BlockSpec((1, heads, RET_DK, RET_DV), lambda bi, h, c: (bi, h, 0, 0))],
        out_shape=[jax.ShapeDtypeStruct((b, s, RET_HEADS * RET_DV), BF16),
                   jax.ShapeDtypeStruct(s0.shape, F32)],
        scratch_shapes=[pltpu.VMEM((heads, RET_DK, RET_DV), F32)],
        compiler_params=_params("arbitrary", "arbitrary", "arbitrary"), name="retention",
    )(dc, q, k, v, g, s0, din, dq, dk)


def _conv_kernel(x_ref, g_ref, wb_ref, wc_ref, wh_ref, cw_ref, buf_ref, y_ref, u_ref, *rest, ts, emit):
    ue_sc = rest[-1]
    s = pl.program_id(2)
    w_refs = (wb_ref, wc_ref, wh_ref)
    if emit:
        ws_sc = rest[-2]

        @pl.when((pl.program_id(1) == 0) & (s == 0))
        def _():
            for t, (w_ref, o_ref) in enumerate(zip(w_refs, rest[:3])):
                ws_sc[t] = w_ref[...].astype(BF16)
                o_ref[...] = ws_sc[t]

        wb, wc, wh = (ws_sc.at[t] for t in range(3))
    else:
        wb, wc, wh = w_refs

    @pl.when(s == 0)
    def _():
        ue_sc[0:CONV_HALO, :] = buf_ref[0]

    xn = _rms(x_ref[0], g_ref[...]).astype(BF16)
    bb = _dot(xn, wb[...])
    u = _dot(xn, wc[...]) * _dot(xn, wh[...])
    u_ref[0] = u
    ue_sc[CONV_HALO:CONV_HALO + ts, :] = u
    ue = ue_sc[...]
    conv = cw_ref[CONV_WIDTH - 1:CONV_WIDTH, :] * u
    for t in range(CONV_WIDTH - 1):
        shift = CONV_WIDTH - 1 - t
        conv = conv + cw_ref[t:t + 1, :] * pltpu.roll(ue, shift, 0)[CONV_HALO:, :]
    y_ref[0] = (bb * conv).astype(BF16)
    ue_sc[0:CONV_HALO, :] = ue_sc[ts:ts + CONV_HALO, :]


def _short_conv(x, gain, views, conv_w, buf, ts, emit=False, tn=512):
    b, s, d = x.shape
    ts = min(ts, s)
    assert s % ts == 0 and d % tn == 0 and all(off % tn == 0 for _, _, off in views)
    nb = d // tn
    w_specs = [pl.BlockSpec((None,) * len(lead) + (d, tn),
                            functools.partial(lambda n, bi, si, lead, o: tuple(lead) + (0, n + o), lead=lead, o=off // tn))
               for _, lead, off in views]
    out_specs = [pl.BlockSpec((1, ts, tn), lambda n, bi, si: (bi, si, n)),
                 pl.BlockSpec((1, ts, tn), lambda n, bi, si: (bi, si, n))]
    out_shape = [jax.ShapeDtypeStruct((b, s, d), BF16), jax.ShapeDtypeStruct((b, s, d), F32)]
    scratch = [pltpu.VMEM((ts + CONV_HALO, tn), F32)]
    if emit:
        out_specs += [pl.BlockSpec((d, tn), lambda n, bi, si: (0, n))] * 3
        out_shape += [jax.ShapeDtypeStruct((d, d), BF16)] * 3
        scratch.insert(0, pltpu.VMEM((3, d, tn), BF16))
    return pl.pallas_call(
        functools.partial(_conv_kernel, ts=ts, emit=emit),
        grid=(nb, b, s // ts),
        in_specs=[pl.BlockSpec((1, ts, d), lambda n, bi, si: (bi, si, 0)),
                  pl.BlockSpec((1, d), lambda n, bi, si: (0, 0))] + w_specs + [
                  pl.BlockSpec((CONV_WIDTH, tn), lambda n, bi, si: (0, n)),
                  pl.BlockSpec((1, CONV_HALO, tn), lambda n, bi, si: (bi, 0, n))],
        out_specs=out_specs, out_shape=out_shape, scratch_shapes=scratch,
        compiler_params=_params("arbitrary", "arbitrary", "arbitrary"), name="short_conv",
    )(x, gain.reshape(1, d), *[w for w, _, _ in views], conv_w, buf)


def _rope_tables(pos):
    inv = ROPE_BASE ** (-jnp.arange(0, RET_DK, 2, dtype=F32) / RET_DK)
    ang = pos.astype(F32)[:, None] * inv[None, :]
    return jnp.cos(ang), jnp.sin(ang)


def _trunk(x, pos0, n_valid, pool_buf, ret_s0, conv_buf, paged, p, wts, ts):
    emit = wts is None
    made = {}
    b, s, d = x.shape
    m = b * s
    depth = p["ffn_norm"].shape[0]
    pools, ks, vs, rets, convs = [], [], [], [], []
    x = x.reshape(m, d)

    def ffn(x, i, k):
        if emit:
            x, made["ffn_w13", i, k], made["ffn_w2", i, k] = _ffn(
                x, p["ffn_norm"][i, k], (p["ffn_w13"], (i, k)), (p["ffn_w2"], (i, k)), emit=True)
            return x
        return _ffn(x, p["ffn_norm"][i, k], wts["ffn_w13", i, k], wts["ffn_w2", i, k])

    def out_proj(y, name, j, x):
        if emit:
            x, wb = _mm_res(y, p[name], (j,), x, 1.0, emit=True, tn=1024, name=name)
            made[name, j] = (wb, ())
            return x
        return _mm_res(y, *wts[name, j], x, 1.0, tn=1024, name=name)

    for i in range(depth):
        j, mixer = i // 4, i % 4
        x = ffn(x, i, 0)
        gain = p["mix_norm"][i]
        if mixer == 0:
            buf = jnp.pad(pool_buf[j], ((0, 0), (POOL_HALO - pool_buf.shape[2], 0), (0, 0)))
            x3, xn = _pool_mixer(x.reshape(b, s, d), gain, buf, p["pool_w"][j].astype(BF16), p["pool_scale"][j],
                                 pos0, ts)
            x = x3.reshape(m, d)
            pools.append(jnp.concatenate([pool_buf[j], xn[:, :n_valid]], axis=1)[:, -pool_buf.shape[2]:])
        elif mixer == 1:
            hd_all = SB_HEADS * SB_HEAD_DIM
            gspec = pl.BlockSpec((1, SB_HEAD_DIM), lambda i_, j_: (0, 0))
            n_h = 512 // SB_HEAD_DIM
            w, lead = (p["sb_w_qkv"], (j,)) if emit else wts["sb_w_qkv", j]
            res = _norm_proj(
                x, gain, w, lead,
                [(hd_all, functools.partial(_ep_sb_q, n_heads=n_h, scale=LOG2_E / math.sqrt(SB_HEAD_DIM)), (BF16,)),
                 (hd_all, functools.partial(_ep_sb_k, n_heads=n_h), (F32, BF16)),
                 (hd_all, _ep_copy2, (F32, BF16))],
                extras=[(p["sb_q_norm"][j].reshape(1, -1), gspec), (p["sb_k_norm"][j].reshape(1, -1), gspec)],
                emit=emit, name="sb_qkv")
            q, k32, k16, v32, v16 = res[:5]
            if emit:
                made["sb_w_qkv", j] = (res[5], ())
            shp = (b, s, SB_HEADS, SB_HEAD_DIM)
            ks.append(k32.reshape(shp)[:, :n_valid])
            vs.append(v32.reshape(shp)[:, :n_valid])
            if paged:
                o = _sb_sample(q.reshape(b, s, hd_all), k32.reshape(shp), v32.reshape(shp),
                               p["cache_sb_k"][j], p["cache_sb_v"][j], p["page_table"], p["sb_bias"][j])
            else:
                o = _sb_prompt(q.reshape(b, s, hd_all), k16.reshape(b, s, hd_all), v16.reshape(b, s, hd_all),
                               p["sb_bias"][j].astype(F32))
            x = out_proj(o.reshape(m, hd_all), "sb_w_o", j, x)
        elif mixer == 2:
            hk, hv = RET_HEADS * RET_DK, RET_HEADS * RET_DV
            pos = pos0 + jnp.arange(s, dtype=jnp.int32)
            cos, sin = (jnp.tile(t, (b, 1)) for t in _rope_tables(pos))
            tspec = pl.BlockSpec((min(1024, m), RET_DK // 2), lambda i_, j_: (i_, 0))
            tn = 1024
            rot = lambda sc: functools.partial(_ep_rotary, n_heads=tn // RET_DK, scale=sc)
            w, lead = (p["ret_w_qkvg"], (j,)) if emit else wts["ret_w_qkvg", j]
            res = _norm_proj(
                x, gain, w, lead,
                [(hk, rot(1.0), (BF16,)), (hk, rot(RET_DK ** -0.5), (BF16,)), (hv, _ep_bf16, (BF16,)),
                 (hv, _ep_silu, (BF16,))],
                extras=[(cos, tspec), (sin, tspec)], emit=emit, tn=tn, name="ret_qkvg")
            q, k, v, g = res[:4]
            if emit:
                made["ret_w_qkvg", j] = (res[4], ())
            chunk = min(256, s)
            c_true = min(chunk, n_valid)
            y, s_new = _retention(q.reshape(b, s, hk), k.reshape(b, s, hk), v.reshape(b, s, hv),
                                  g.reshape(b, s, hv), ret_s0[j].astype(F32), chunk, c_true)
            rets.append(s_new)
            x = out_proj(y.reshape(m, hv), "ret_w_o", j, x)
        else:
            cb = conv_buf[j]
            buf = jnp.pad(cb, ((0, 0), (CONV_HALO - cb.shape[1], 0), (0, 0)))
            views = [(p["conv_w_in"], (j,), t * d) for t in range(3)] if emit else wts["conv_w_in", j]
            res = _short_conv(x.reshape(b, s, d), gain, views, p["conv_w"][j], buf, 2 * ts, emit=emit)
            if emit:
                made["conv_w_in", j] = [(wb, (), 0) for wb in res[2:]]
            convs.append(jnp.concatenate([cb, res[1][:, :n_valid]], axis=1)[:, -cb.shape[1]:])
            x = out_proj(res[0].reshape(m, d), "conv_w_out", j, x)
        x = ffn(x, i, 1)
    x = x.reshape(b, s, d)[:, :n_valid]
    outs = (x, jnp.stack(pools), jnp.stack(ks), jnp.stack(vs), jnp.stack(rets), jnp.stack(convs))
    return outs, made


def kernel(x_prompt, x_sample, state_pool, cache_sb_k, cache_sb_v, page_table, state_ret, state_conv,
           ffn_norm, ffn_w13, ffn_w2, mix_norm, pool_w, pool_scale, sb_w_qkv, sb_q_norm, sb_k_norm,
           sb_bias, sb_w_o, ret_w_qkvg, ret_w_o, conv_w_in, conv_w, conv_w_out):
    p = dict(
        ffn_norm=ffn_norm, ffn_w13=ffn_w13, ffn_w2=ffn_w2, mix_norm=mix_norm, pool_w=pool_w,
        pool_scale=pool_scale, sb_w_qkv=sb_w_qkv, sb_q_norm=sb_q_norm, sb_k_norm=sb_k_norm, sb_bias=sb_bias,
        sb_w_o=sb_w_o, ret_w_qkvg=ret_w_qkvg, ret_w_o=ret_w_o, conv_w_in=conv_w_in, conv_w=conv_w,
        conv_w_out=conv_w_out, cache_sb_k=cache_sb_k, cache_sb_v=cache_sb_v, page_table=page_table)

    bs, ss, _ = x_sample.shape
    past_len = page_table.shape[1] * cache_sb_k.shape[2]
    xs = jnp.pad(x_sample, ((0, 0), (0, SAMPLE_ROWS - ss), (0, 0)))
    (y_sample, pool_s, k_s, v_s, ret_s, conv_s), wts = _trunk(
        xs, past_len, ss, state_pool, state_ret, state_conv, True, p, None, SAMPLE_ROWS)

    bp, sp, d = x_prompt.shape
    (y_prompt, pool_p, k_p, v_p, ret_p, conv_p), _ = _trunk(
        x_prompt, 0, sp,
        jnp.zeros((state_pool.shape[0], bp) + state_pool.shape[2:], F32),
        jnp.zeros((state_ret.shape[0], bp) + state_ret.shape[2:], F32),
        jnp.zeros((state_conv.shape[0], bp) + state_conv.shape[2:], F32),
        False, p, wts, 256)
    return (y_prompt, y_sample, pool_p, pool_s, k_p, v_p, k_s, v_s, ret_p, ret_s, conv_p, conv_s)
```
